```python
import math
import jax, jax.numpy as jnp
from jax import lax
import numpy as np

D_MODEL = 1024
BATCH = 8
SEQ = 2048
DEPTH = 2
DEC_BATCH = 128
DEC_SEQ = 1
PAST_LEN = 2048
PAGE_SIZE = 128

DIFF_HEADS = 4
DIFF_DK = D_MODEL // 16
DIFF_DV = 2 * DIFF_DK
GLA_HEADS = 4
GLA_DV = (D_MODEL - DIFF_HEADS * DIFF_DV) // GLA_HEADS
GLA_DK = GLA_DV // 2
GLA_GATE_RANK = 16
GLA_GATE_NORM = 16.0
GLA_CHUNK = 64
Q_BLOCK = 128
ROPE_THETA = 10000.0
D_FF = -(-8 * D_MODEL // (3 * 256)) * 256
EPS = 1e-6
COL_SIZES = (DIFF_HEADS * 2 * DIFF_DK, DIFF_HEADS * 2 * DIFF_DK, DIFF_HEADS * DIFF_DV,
             GLA_HEADS * GLA_DK, GLA_HEADS * GLA_DK, GLA_HEADS * GLA_DV, GLA_HEADS * GLA_DV,
             GLA_GATE_RANK)
D_IN = sum(COL_SIZES)
MIX_WIDTH = DIFF_HEADS * DIFF_DV + GLA_HEADS * GLA_DV

kernel_name = 'hybrid_diffattn_gla_decode_step'


def rmsnorm(x, g):
    xf = x.astype(jnp.float32)
    y = xf * lax.rsqrt(jnp.mean(xf * xf, axis=-1, keepdims=True) + EPS)
    return (y * g.astype(jnp.float32)).astype(x.dtype)


def rope(x, pos):
    half = x.shape[-1] // 2
    freqs = ROPE_THETA ** (-jnp.arange(half, dtype=jnp.float32) / half)
    ang = pos.astype(jnp.float32)[:, None] * freqs[None, :]
    shape = (1, pos.shape[0]) + (1,) * (x.ndim - 3) + (half,)
    cos = jnp.cos(ang).reshape(shape)
    sin = jnp.sin(ang).reshape(shape)
    xf = x.astype(jnp.float32)
    x1, x2 = xf[..., :half], xf[..., half:]
    return jnp.concatenate([x1 * cos - x2 * sin, x2 * cos + x1 * sin], axis=-1).astype(x.dtype)


def project(n, w_in_l, q_norm_l, k_norm_l, w_a2_l, b_a_l, pos):
    B, T, _ = n.shape
    z = n @ w_in_l
    cuts = np.cumsum(COL_SIZES)[:-1].tolist()
    dq, dk, dv, gq, gk, gv, gg, gr = jnp.split(z, cuts, axis=-1)
    q = rope(rmsnorm(dq.reshape(B, T, DIFF_HEADS, 2, DIFF_DK), q_norm_l), pos)
    k = rope(rmsnorm(dk.reshape(B, T, DIFF_HEADS, 2, DIFF_DK), k_norm_l), pos)
    v = dv.reshape(B, T, DIFF_HEADS, DIFF_DV)
    gq = gq.reshape(B, T, GLA_HEADS, GLA_DK) * (GLA_DK ** -0.5)
    gk = gk.reshape(B, T, GLA_HEADS, GLA_DK)
    gv = gv.reshape(B, T, GLA_HEADS, GLA_DV)
    glog = jax.nn.log_sigmoid((gr @ w_a2_l + b_a_l).astype(jnp.float32)) / GLA_GATE_NORM
    glog = glog.reshape(B, T, GLA_HEADS, GLA_DK)
    return q, k, v, gq, gk, gv, glog, gg


def diff_lambda(lqk_l, lam_init):
    lf = lqk_l.astype(jnp.float32)
    return jnp.exp(jnp.sum(lf[0] * lf[1])) - jnp.exp(jnp.sum(lf[2] * lf[3])) + lam_init


def diff_attn_prompt(q, k, v, lam):
    B, S = q.shape[:2]
    nb = S // Q_BLOCK
    scale = DIFF_DK ** -0.5
    qb = q.reshape(B, nb, Q_BLOCK, DIFF_HEADS, 2, DIFF_DK).transpose(1, 0, 2, 3, 4, 5)
    kpos = jnp.arange(S)

    def one_block(args):
        qi, i = args
        s = jnp.einsum('bqhmd,bkhmd->bhmqk', qi, k).astype(jnp.float32) * scale
        qpos = i * Q_BLOCK + jnp.arange(Q_BLOCK)
        mask = kpos[None, :] <= qpos[:, None]
        p = jax.nn.softmax(jnp.where(mask, s, -jnp.inf), axis=-1)
        w = p[:, :, 0] - lam * p[:, :, 1]
        return jnp.einsum('bhqk,bkhv->bqhv', w.astype(v.dtype), v)

    o = lax.map(one_block, (qb, jnp.arange(nb)))
    return o.transpose(1, 0, 2, 3, 4).reshape(B, S, DIFF_HEADS, DIFF_DV)


def diff_attn_sample(q, k_new, v_new, k_past, v_past, lam):
    T = q.shape[1]
    P = k_past.shape[1]
    scale = DIFF_DK ** -0.5
    s_past = jnp.einsum('bqhmd,bkhmd->bhmqk', q, k_past).astype(jnp.float32) * scale
    s_new = jnp.einsum('bqhmd,bkhmd->bhmqk', q, k_new).astype(jnp.float32) * scale
    causal = jnp.tril(jnp.ones((T, T), dtype=bool))
    s_new = jnp.where(causal, s_new, -jnp.inf)
    p = jax.nn.softmax(jnp.concatenate([s_past, s_new], axis=-1), axis=-1)
    w = (p[:, :, 0] - lam * p[:, :, 1]).astype(v_new.dtype)
    return (jnp.einsum('bhqk,bkhv->bqhv', w[..., :P], v_past)
            + jnp.einsum('bhqk,bkhv->bqhv', w[..., P:], v_new))


def gla_chunk(S0, q, k, v, g):
    L = q.shape[1]
    qf, kf, vf = q.astype(jnp.float32), k.astype(jnp.float32), v.astype(jnp.float32)
    b = jnp.cumsum(g.astype(jnp.float32), axis=1)
    o_inter = jnp.einsum('blhk,bhkv->blhv', qf * jnp.exp(b), S0)
    mask = jnp.tril(jnp.ones((L, L), dtype=bool))[None, :, :, None, None]
    decay = jnp.exp(jnp.where(mask, b[:, :, None] - b[:, None, :], -jnp.inf))
    A = jnp.einsum('bthk,bshk,btshk->bths', qf, kf, decay)
    o = o_inter + jnp.einsum('bths,bshv->bthv', A, vf)
    b_last = b[:, -1]
    S_new = (jnp.exp(b_last)[..., None] * S0
             + jnp.einsum('bshk,bshv->bhkv', kf * jnp.exp(b_last[:, None] - b), vf))
    return o, S_new


def gla_prompt(q, k, v, g):
    B, S = q.shape[:2]
    nc = S // GLA_CHUNK

    def to_chunks(a):
        return a.reshape((B, nc, GLA_CHUNK) + a.shape[2:]).transpose(1, 0, 2, 3, 4)

    def step(state, xs):
        o, state = gla_chunk(state, *xs)
        return state, o

    S0 = jnp.zeros((B, GLA_HEADS, GLA_DK, GLA_DV), jnp.float32)
    S_fin, o = lax.scan(step, S0, (to_chunks(q), to_chunks(k), to_chunks(v), to_chunks(g)))
    o = o.transpose(1, 0, 2, 3, 4).reshape(B, S, GLA_HEADS, GLA_DV)
    return o, S_fin


def merge(o_diff, o_gla, gate, subln_l, gla_norm_l, lam_init, w_out_l):
    B, T = o_diff.shape[:2]
    od = rmsnorm(o_diff, subln_l) * (1.0 - lam_init)
    og = rmsnorm(o_gla, gla_norm_l).reshape(B, T, GLA_HEADS * GLA_DV) * jax.nn.silu(gate)
    o = jnp.concatenate([od.reshape(B, T, DIFF_HEADS * DIFF_DV), og], axis=-1)
    return o @ w_out_l


def ffn(h, norm2_l, w_gu_l, w_down_l):
    n = rmsnorm(h, norm2_l)
    a, b = jnp.split(n @ w_gu_l, 2, axis=-1)
    return (jax.nn.silu(a) * b) @ w_down_l


def setup_inputs(seed: int = 0) -> dict:
    key = jax.random.key(seed)
    ks = jax.random.split(key, 20)
    n_pages = PAST_LEN // PAGE_SIZE
    n_used = DEC_BATCH * n_pages
    n_phys = n_used + max(1, n_used // 4)
    page_table = jax.random.permutation(ks[0], n_phys)[:n_used].reshape(DEC_BATCH, n_pages).astype(jnp.int32)
    f32 = jnp.float32
    nrm = lambda k, s: jax.random.normal(k, s, f32)
    gain = lambda k, s: 1.0 + 0.02 * nrm(k, s)
    return {
        'x_prompt': nrm(ks[1], (BATCH, SEQ, D_MODEL)),
        'x_sample': nrm(ks[2], (DEC_BATCH, DEC_SEQ, D_MODEL)),
        'cache_k': nrm(ks[3], (DEPTH, n_phys, PAGE_SIZE, DIFF_HEADS, 2, DIFF_DK)),
        'cache_v': nrm(ks[4], (DEPTH, n_phys, PAGE_SIZE, DIFF_HEADS, DIFF_DV)),
        'state_gla': nrm(ks[5], (DEPTH, DEC_BATCH, GLA_HEADS, GLA_DK, GLA_DV)),
        'page_table': page_table,
        'norm1': gain(ks[6], (DEPTH, D_MODEL)),
        'w_in': nrm(ks[7], (DEPTH, D_MODEL, D_IN)) * D_MODEL ** -0.5,
        'q_norm': gain(ks[8], (DEPTH, DIFF_DK)),
        'k_norm': gain(ks[9], (DEPTH, DIFF_DK)),
        'lambda_qk': 0.1 * nrm(ks[10], (DEPTH, 4, DIFF_DK)),
        'subln': gain(ks[11], (DEPTH, DIFF_DV)),
        'w_a2': nrm(ks[12], (DEPTH, GLA_GATE_RANK, GLA_HEADS * GLA_DK)) * GLA_GATE_RANK ** -0.5,
        'b_a': 0.1 * nrm(ks[13], (DEPTH, GLA_HEADS * GLA_DK)),
        'gla_norm': gain(ks[14], (DEPTH, GLA_DV)),
        'w_out': nrm(ks[15], (DEPTH, MIX_WIDTH, D_MODEL)) * MIX_WIDTH ** -0.5,
        'norm2': gain(ks[16], (DEPTH, D_MODEL)),
        'w_gu': nrm(ks[17], (DEPTH, D_MODEL, 2 * D_FF)) * D_MODEL ** -0.5,
        'w_down': nrm(ks[18], (DEPTH, D_FF, D_MODEL)) * D_FF ** -0.5,
    }


def reference(x_prompt, x_sample, cache_k, cache_v, state_gla, page_table, norm1, w_in, q_norm, k_norm,
              lambda_qk, subln, w_a2, b_a, gla_norm, w_out, norm2, w_gu, w_down):
    yp, ys = x_prompt, x_sample
    S, T = yp.shape[1], ys.shape[1]
    DB = ys.shape[0]
    pos_p = jnp.arange(S)
    pos_s = PAST_LEN + jnp.arange(T)
    kp, vp, sp, k_s, v_s, s_s = [], [], [], [], [], []
    for l in range(DEPTH):
        lam_init = 0.8 - 0.6 * math.exp(-0.3 * l)
        lam = diff_lambda(lambda_qk[l], lam_init)
        n = rmsnorm(yp, norm1[l])
        q, k, v, gq, gk, gv, glog, gg = project(n, w_in[l], q_norm[l], k_norm[l], w_a2[l], b_a[l], pos_p)
        od = diff_attn_prompt(q, k, v, lam)
        og, s_fin = gla_prompt(gq, gk, gv, glog)
        yp = yp + merge(od, og.astype(yp.dtype), gg, subln[l], gla_norm[l], lam_init, w_out[l])
        yp = yp + ffn(yp, norm2[l], w_gu[l], w_down[l])
        kp.append(k)
        vp.append(v)
        sp.append(s_fin.astype(state_gla.dtype))
        n = rmsnorm(ys, norm1[l])
        q, k, v, gq, gk, gv, glog, gg = project(n, w_in[l], q_norm[l], k_norm[l], w_a2[l], b_a[l], pos_s)
        k_past = cache_k[l, page_table].reshape(DB, -1, DIFF_HEADS, 2, DIFF_DK)
        v_past = cache_v[l, page_table].reshape(DB, -1, DIFF_HEADS, DIFF_DV)
        od = diff_attn_sample(q, k, v, k_past, v_past, lam)
        og, s_new = gla_chunk(state_gla[l].astype(jnp.float32), gq, gk, gv, glog)
        ys = ys + merge(od, og.astype(ys.dtype), gg, subln[l], gla_norm[l], lam_init, w_out[l])
        ys = ys + ffn(ys, norm2[l], w_gu[l], w_down[l])
        k_s.append(k)
        v_s.append(v)
        s_s.append(s_new.astype(state_gla.dtype))
    return (yp, ys, jnp.stack(kp), jnp.stack(vp), jnp.stack(sp), jnp.stack(k_s), jnp.stack(v_s), jnp.stack(s_s))
```

```python
import functools
import math

import jax
import jax.numpy as jnp
from jax import lax
from jax.experimental import pallas as pl
from jax.experimental.pallas import tpu as pltpu

f32 = jnp.float32
bf16 = jnp.bfloat16

DIFF_HEADS = 4
DIFF_DK = 64
DIFF_DV = 128
GLA_HEADS = 4
GLA_DK = 64
GLA_DV = 128
GLA_GATE_NORM = 16.0
GLA_CHUNK = 64
ROPE_THETA = 10000.0
EPS = 1e-6

QK_W = DIFF_HEADS * 2 * DIFF_DK
DV_W = DIFF_HEADS * DIFF_DV
GK_W = GLA_HEADS * GLA_DK
GV_W = GLA_HEADS * GLA_DV
MAIN_W = 2 * QK_W + DV_W + 2 * GK_W + 2 * GV_W

LANES = 128
VMEM_LIMIT = 56 * 1024 * 1024


def _cparams(sem):
    return pltpu.CompilerParams(dimension_semantics=sem, vmem_limit_bytes=VMEM_LIMIT)


def _const_spec(shape):
    nd = len(shape)
    return pl.BlockSpec(shape, lambda *_: (0,) * nd, pipeline_mode=pl.Buffered(1))


def _group_scale(z, width):
    lane = lax.broadcasted_iota(jnp.int32, (z.shape[0], LANES), 1)
    cols = []
    for c in range(z.shape[1] // LANES):
        zc = z[:, c * LANES:(c + 1) * LANES]
        zz = zc * zc
        if width == LANES:
            cols.append(jnp.broadcast_to(lax.rsqrt(jnp.mean(zz, axis=-1, keepdims=True) + EPS), zc.shape))
        else:
            lo = lane < width
            s_lo = jnp.sum(jnp.where(lo, zz, 0.0), axis=-1, keepdims=True)
            s_hi = jnp.sum(jnp.where(lo, 0.0, zz), axis=-1, keepdims=True)
            r_lo = lax.rsqrt(s_lo * (1.0 / width) + EPS)
            r_hi = lax.rsqrt(s_hi * (1.0 / width) + EPS)
            cols.append(jnp.where(lo, r_lo, r_hi))
    return jnp.concatenate(cols, axis=-1)


def _rope(z, cos, sin_signed):
    lane = lax.broadcasted_iota(jnp.int32, (z.shape[0], LANES), 1)
    first_half = (lane % DIFF_DK) < (DIFF_DK // 2)
    cols = []
    for c in range(z.shape[1] // LANES):
        zc = z[:, c * LANES:(c + 1) * LANES]
        partner = jnp.where(first_half, pltpu.roll(zc, LANES - DIFF_DK // 2, 1), pltpu.roll(zc, DIFF_DK // 2, 1))
        cols.append(zc * cos + partner * sin_signed)
    return jnp.concatenate(cols, axis=-1)


def _diff_lambda(lqk, lam_init):
    a = jnp.sum(lqk[0:1, :] * lqk[1:2, :], axis=-1, keepdims=True)
    b = jnp.sum(lqk[2:3, :] * lqk[3:4, :], axis=-1, keepdims=True)
    return jnp.exp(a) - jnp.exp(b) + lam_init


def _proj_kernel(x_ref, g1_ref, w_ref, wr_ref, wa2_ref, ba_ref, qn_ref, kn_ref, cos_ref, sin_ref,
                 q_ref, k_ref, v_ref, gq_ref, gk_ref, gv_ref, gl_ref, gg_ref):
    x = x_ref[...]
    n = x * lax.rsqrt(jnp.mean(x * x, axis=-1, keepdims=True) + EPS) * g1_ref[...]
    nb = n.astype(bf16)

    def seg(lo, width):
        return jnp.dot(nb, w_ref[:, lo:lo + width], preferred_element_type=f32)

    cos = cos_ref[...]
    sin = sin_ref[...]
    lane = lax.broadcasted_iota(jnp.int32, (x.shape[0], LANES), 1)
    first_map = lane < DIFF_DK

    zq = seg(0, QK_W)
    q = _rope(zq * _group_scale(zq, DIFF_DK) * qn_ref[...], cos, sin) * (DIFF_DK ** -0.5)
    for h in range(DIFF_HEADS):
        qh = q[:, h * LANES:(h + 1) * LANES]
        q_ref[:, (2 * h) * LANES:(2 * h + 1) * LANES] = jnp.where(first_map, qh, 0.0).astype(q_ref.dtype)
        q_ref[:, (2 * h + 1) * LANES:(2 * h + 2) * LANES] = jnp.where(first_map, 0.0, qh).astype(q_ref.dtype)

    zk = seg(QK_W, QK_W)
    k_ref[...] = _rope(zk * _group_scale(zk, DIFF_DK) * kn_ref[...], cos, sin)
    v_ref[...] = seg(2 * QK_W, DV_W)

    off = 2 * QK_W + DV_W
    gq_ref[...] = seg(off, GK_W) * (GLA_DK ** -0.5)
    gk_ref[...] = seg(off + GK_W, GK_W)
    gv_ref[...] = seg(off + 2 * GK_W, GV_W).astype(gv_ref.dtype)
    gg_ref[...] = seg(off + 2 * GK_W + GV_W, GV_W)

    r = jnp.dot(nb, wr_ref[...], preferred_element_type=f32)
    a = jnp.dot(r.astype(bf16), wa2_ref[...], preferred_element_type=f32) + ba_ref[...]
    gl_ref[...] = (jnp.minimum(a, 0.0) - jnp.log1p(jnp.exp(-jnp.abs(a)))) * (1.0 / GLA_GATE_NORM)


def _proj(x2d, g1, w_main, w_r, w_a2, b_a, qn, kn, cos_t, sin_t, *, tm, narrow):
    n, d = x2d.shape
    nt = cos_t.shape[0] // tm
    row = lambda w: pl.BlockSpec((tm, w), lambda i: (i, 0))
    tab = pl.BlockSpec((tm, LANES), lambda i: (i % nt, 0))
    qdt = bf16 if narrow else f32
    out_shape = (
        jax.ShapeDtypeStruct((n, 2 * QK_W), qdt),
        jax.ShapeDtypeStruct((n, QK_W), f32),
        jax.ShapeDtypeStruct((n, DV_W), f32),
        jax.ShapeDtypeStruct((n, GK_W), f32),
        jax.ShapeDtypeStruct((n, GK_W), f32),
        jax.ShapeDtypeStruct((n, GV_W), qdt),
        jax.ShapeDtypeStruct((n, GK_W), f32),
        jax.ShapeDtypeStruct((n, GV_W), f32),
    )
    return pl.pallas_call(
        _proj_kernel,
        grid=(n // tm,),
        in_specs=[row(d), _const_spec((1, d)), _const_spec(w_main.shape), _const_spec(w_r.shape),
                  _const_spec(w_a2.shape), _const_spec((1, GK_W)), _const_spec((1, QK_W)), _const_spec((1, QK_W)),
                  tab, tab],
        out_specs=(row(2 * QK_W), row(QK_W), row(DV_W), row(GK_W), row(GK_W), row(GV_W), row(GK_W), row(GV_W)),
        out_shape=out_shape,
        compiler_params=_cparams(("parallel",)),
        name="proj",
    )(x2d, g1, w_main, w_r, w_a2, b_a, qn, kn, cos_t, sin_t)


def _attn_kernel(q_ref, k_ref, v_ref, lqk_ref, o_ref, kb_ref, vb_ref, m_ref, acc_ref, *, tq, tk, lam_init):
    qi = pl.program_id(2)

    @pl.when(qi == 0)
    def _():
        kb_ref[...] = k_ref[...].astype(bf16)
        vb_ref[:, :DIFF_DV] = v_ref[...].astype(bf16)
        vb_ref[:, DIFF_DV:] = jnp.ones((vb_ref.shape[0], LANES), bf16)

    qs = jnp.concatenate([q_ref[:, :LANES], q_ref[:, LANES:]], axis=0)
    m_ref[...] = jnp.full(m_ref.shape, -jnp.inf, f32)
    acc_ref[...] = jnp.zeros(acc_ref.shape, f32)

    def tile(kt, diag_j):
        r0 = pl.multiple_of(kt * tk, tk)
        s = lax.dot_general(qs, kb_ref[pl.ds(r0, tk), :], (((1,), (1,)), ((), ())), preferred_element_type=f32)
        if diag_j is not None:
            row = lax.broadcasted_iota(jnp.int32, (2 * tq, tk), 0) % tq
            col = lax.broadcasted_iota(jnp.int32, (2 * tq, tk), 1) + diag_j * tk
            s = jnp.where(col <= row, s, -jnp.inf)
        m_old = m_ref[...]
        m_new = jnp.maximum(m_old, jnp.max(s, axis=-1, keepdims=True))
        p = jnp.exp(s - m_new[:, :1])
        pv = jnp.dot(p.astype(bf16), vb_ref[pl.ds(r0, tk), :], preferred_element_type=f32)
        acc_ref[...] = acc_ref[...] * jnp.exp(m_old - m_new)[:, :1] + pv
        m_ref[...] = m_new

    def body(kt, carry):
        tile(kt, None)
        return carry

    nfull = qi * (tq // tk)
    lax.fori_loop(0, nfull, body, 0)
    for j in range(tq // tk):
        tile(nfull + j, j)

    acc = acc_ref[...]
    lam = _diff_lambda(lqk_ref[...], lam_init)
    o1 = acc[:tq, :DIFF_DV] / acc[:tq, DIFF_DV:]
    o2 = acc[tq:, :DIFF_DV] / acc[tq:, DIFF_DV:]
    o_ref[...] = o1 - lam * o2


def _attn_prompt(q, k, v, lqk, *, batch, seq, tq, tk, lam_init):
    nq = seq // tq
    return pl.pallas_call(
        functools.partial(_attn_kernel, tq=tq, tk=tk, lam_init=lam_init),
        grid=(batch, DIFF_HEADS, nq),
        in_specs=[pl.BlockSpec((tq, 2 * LANES), lambda b, h, i: (b * nq + i, h)),
                  pl.BlockSpec((seq, LANES), lambda b, h, i: (b, h)),
                  pl.BlockSpec((seq, LANES), lambda b, h, i: (b, h)),
                  _const_spec(lqk.shape)],
        out_specs=pl.BlockSpec((tq, DIFF_DV), lambda b, h, i: (b * nq + i, h)),
        out_shape=jax.ShapeDtypeStruct((batch * seq, DV_W), f32),
        scratch_shapes=[pltpu.VMEM((seq, LANES), bf16), pltpu.VMEM((seq, 2 * LANES), bf16),
                        pltpu.VMEM((2 * tq, LANES), f32), pltpu.VMEM((2 * tq, 2 * LANES), f32)],
        compiler_params=_cparams(("parallel", "parallel", "arbitrary")),
        name="attn_prompt",
    )(q, k, v, lqk)


def _cumsum_rows(g, tril):
    g1 = g.astype(bf16)
    r1 = g - g1.astype(f32)
    g2 = r1.astype(bf16)
    g3 = (r1 - g2.astype(f32)).astype(bf16)
    dot = lambda t: jnp.dot(tril, t, preferred_element_type=f32)
    return dot(g1) + dot(g2) + dot(g3)


def _gla_kernel(q_ref, k_ref, v_ref, g_ref, o_ref, s_ref, st_ref, *, tg):
    step = pl.program_id(1)

    @pl.when(step == 0)
    def _():
        st_ref[...] = jnp.zeros(st_ref.shape, f32)

    c = GLA_CHUNK
    ri = lax.broadcasted_iota(jnp.int32, (c, c), 0)
    ci = lax.broadcasted_iota(jnp.int32, (c, c), 1)
    causal = ri >= ci
    tril = jnp.where(causal, 1.0, 0.0).astype(bf16)
    lane = lax.broadcasted_iota(jnp.int32, (c, LANES), 1)
    head0 = lane < GLA_DK
    lane_sq = lax.broadcasted_iota(jnp.int32, (LANES, LANES), 1) < GLA_DK

    def chunk(ic, carry):
        r0 = pl.multiple_of(ic * c, c)
        rows = pl.ds(r0, c)
        b = _cumsum_rows(g_ref[rows, :], tril)
        q = q_ref[rows, :]
        k = k_ref[rows, :]
        b_mid = b[c // 2 - 1:c // 2, :]
        b_last = b[c - 1:c, :]
        qe = q * jnp.exp(b - b_mid)
        ke = (k * jnp.exp(b_mid - b)).astype(bf16)
        qb = q * jnp.exp(b)
        kd = (k * jnp.exp(b_last - b)).astype(bf16)
        e_last = jnp.exp(b_last)
        for p in range(GLA_HEADS // 2):
            sl = slice(p * LANES, (p + 1) * LANES)
            st = st_ref[p]
            stb = st.astype(bf16)
            upd = []
            for j in range(2):
                h = 2 * p + j
                mine = head0 if j == 0 else jnp.logical_not(head0)
                qe_h = jnp.where(mine, qe[:, sl], 0.0).astype(bf16)
                qb_h = jnp.where(mine, qb[:, sl], 0.0).astype(bf16)
                v_h = v_ref[rows, h * GLA_DV:(h + 1) * GLA_DV]
                a = lax.dot_general(qe_h, ke[:, sl], (((1,), (1,)), ((), ())), preferred_element_type=f32)
                a = jnp.where(causal, a, 0.0).astype(bf16)
                o = jnp.dot(a, v_h, preferred_element_type=f32)
                o = o + lax.dot_general(qb_h, stb, (((1,), (1,)), ((), ())), preferred_element_type=f32)
                o_ref[rows, h * GLA_DV:(h + 1) * GLA_DV] = o
                upd.append(lax.dot_general(v_h, kd[:, sl], (((0,), (0,)), ((), ())), preferred_element_type=f32))
            st_ref[p] = st * e_last[:, sl] + jnp.where(lane_sq, upd[0], upd[1])
        return carry

    lax.fori_loop(0, tg // c, chunk, 0)

    @pl.when(step == pl.num_programs(1) - 1)
    def _():
        for p in range(GLA_HEADS // 2):
            t = st_ref[p].T
            s_ref[0, 2 * p] = t[:GLA_DK, :]
            s_ref[0, 2 * p + 1] = t[GLA_DK:, :]


def _gla_prompt(gq, gk, gv, gl, *, batch, seq, tg):
    ns = seq // tg
    row = lambda w: pl.BlockSpec((tg, w), lambda b, i: (b * ns + i, 0))
    return pl.pallas_call(
        functools.partial(_gla_kernel, tg=tg),
        grid=(batch, ns),
        in_specs=[row(GK_W), row(GK_W), row(GV_W), row(GK_W)],
        out_specs=(row(GV_W), pl.BlockSpec((1, GLA_HEADS, GLA_DK, GLA_DV), lambda b, i: (b, 0, 0, 0))),
        out_shape=(jax.ShapeDtypeStruct((batch * seq, GV_W), f32),
                   jax.ShapeDtypeStruct((batch, GLA_HEADS, GLA_DK, GLA_DV), f32)),
        scratch_shapes=[pltpu.VMEM((GLA_HEADS // 2, LANES, LANES), f32)],
        compiler_params=_cparams(("parallel", "arbitrary")),
        name="gla_prompt",
    )(gq, gk, gv, gl)


def _post_kernel(x_ref, od_ref, og_ref, gg_ref, sub_ref, gn_ref, wo_ref, n2_ref, wgu_ref, wd_ref, y_ref,
                 *, lam_init, d_ff, ff_chunk):
    od = od_ref[...]
    odn = od * _group_scale(od, DIFF_DV) * sub_ref[...] * (1.0 - lam_init)
    og = og_ref[...]
    ogn = og * _group_scale(og, GLA_DV) * gn_ref[...]
    gg = gg_ref[...]
    ogn = ogn * (gg * jax.nn.sigmoid(gg))
    mix = jnp.concatenate([odn, ogn], axis=-1).astype(bf16)
    y = x_ref[...] + jnp.dot(mix, wo_ref[...], preferred_element_type=f32)

    n2 = (y * lax.rsqrt(jnp.mean(y * y, axis=-1, keepdims=True) + EPS) * n2_ref[...]).astype(bf16)
    acc = y
    for c0 in range(0, d_ff, ff_chunk):
        a = jnp.dot(n2, wgu_ref[:, c0:c0 + ff_chunk], preferred_element_type=f32)
        b = jnp.dot(n2, wgu_ref[:, d_ff + c0:d_ff + c0 + ff_chunk], preferred_element_type=f32)
        hid = (a * jax.nn.sigmoid(a) * b).astype(bf16)
        acc = acc + jnp.dot(hid, wd_ref[c0:c0 + ff_chunk, :], preferred_element_type=f32)
    y_ref[...] = acc


def _post(x2d, od, og, gg, sub, gn, w_out, n2, w_gu, w_down, *, tm, lam_init):
    n, d = x2d.shape
    d_ff = w_down.shape[0]
    row = lambda w: pl.BlockSpec((tm, w), lambda i: (i, 0))
    return pl.pallas_call(
        functools.partial(_post_kernel, lam_init=lam_init, d_ff=d_ff, ff_chunk=2 * LANES),
        grid=(n // tm,),
        in_specs=[row(d), row(DV_W), row(GV_W), row(GV_W), _const_spec((1, DV_W)), _const_spec((1, GV_W)),
                  _const_spec(w_out.shape), _const_spec((1, d)), _const_spec(w_gu.shape), _const_spec(w_down.shape)],
        out_specs=row(d),
        out_shape=jax.ShapeDtypeStruct((n, d), f32),
        compiler_params=_cparams(("parallel",)),
        name="post",
    )(x2d, od, og, gg, sub, gn, w_out, n2, w_gu, w_down)


def _attn_sample_kernel(pt_ref, q_ref, kn_ref, vn_ref, lqk_ref, ck_hbm, cv_hbm, o_ref, kbuf, vbuf, sem,
                        *, layer, n_pages, page, lam_init):
    b = pl.program_id(0)
    nb = pl.num_programs(0)

    def copies(sample, slot):
        out = []
        for j in range(n_pages):
            phys = pt_ref[sample * n_pages + j]
            dst = pl.ds(j * page, page)
            out.append(pltpu.make_async_copy(ck_hbm.at[layer, phys], kbuf.at[slot, dst, :], sem.at[0, slot]))
            out.append(pltpu.make_async_copy(cv_hbm.at[layer, phys], vbuf.at[slot, dst, :], sem.at[1, slot]))
        return out

    @pl.when(b == 0)
    def _():
        for cp in copies(0, 0):
            cp.start()

    slot = b % 2

    @pl.when(b + 1 < nb)
    def _():
        for cp in copies(b + 1, 1 - slot):
            cp.start()

    for cp in copies(b, slot):
        cp.wait()

    q = q_ref[0]
    rid = lax.broadcasted_iota(jnp.int32, (2 * DIFF_HEADS, LANES), 0)
    blocks = []
    for h in range(DIFF_HEADS):
        q1 = q[:, (2 * h) * LANES:(2 * h + 1) * LANES]
        q2 = q[:, (2 * h + 1) * LANES:(2 * h + 2) * LANES]
        blocks.append(jnp.where(rid == 2 * h, q1, jnp.where(rid == 2 * h + 1, q2, 0.0)))
    qrows = jnp.concatenate(blocks, axis=-1)

    kb = kbuf[slot].astype(bf16)
    s = lax.dot_general(qrows.astype(bf16), kb, (((1,), (1,)), ((), ())), preferred_element_type=f32)
    s_new = jnp.sum(qrows * kn_ref[0], axis=-1, keepdims=True)
    m = jnp.maximum(jnp.max(s, axis=-1, keepdims=True), s_new)
    p = jnp.exp(s - m)
    p_new = jnp.exp(s_new - m)
    inv_l = 1.0 / (jnp.sum(p, axis=-1, keepdims=True) + p_new)
    vb = vbuf[slot].astype(bf16)
    o = jnp.dot(p.astype(bf16), vb, preferred_element_type=f32) + p_new * vn_ref[0]
    o = o * inv_l
    lam = _diff_lambda(lqk_ref[...], lam_init)
    heads = []
    for h in range(DIFF_HEADS):
        sl = slice(h * DIFF_DV, (h + 1) * DIFF_DV)
        heads.append(o[2 * h:2 * h + 1, sl] - lam * o[2 * h + 1:2 * h + 2, sl])
    o_ref[0] = jnp.concatenate(heads, axis=-1)


def _attn_sample(page_table, q, k_new, v_new, lqk, cache_k, cache_v, *, layer, lam_init):
    nb, n_pages = page_table.shape
    page = cache_k.shape[2]
    past = n_pages * page
    row3 = lambda w: pl.BlockSpec((1, 1, w), lambda b, pt: (b, 0, 0))
    grid_spec = pltpu.PrefetchScalarGridSpec(
        num_scalar_prefetch=1,
        grid=(nb,),
        in_specs=[row3(2 * QK_W), row3(QK_W), row3(DV_W),
                  pl.BlockSpec(lqk.shape, lambda b, pt: (0, 0)),
                  pl.BlockSpec(memory_space=pl.ANY), pl.BlockSpec(memory_space=pl.ANY)],
        out_specs=row3(DV_W),
        scratch_shapes=[pltpu.VMEM((2, past, QK_W), f32), pltpu.VMEM((2, past, DV_W), f32),
                        pltpu.SemaphoreType.DMA((2, 2))],
    )
    out = pl.pallas_call(
        functools.partial(_attn_sample_kernel, layer=layer, n_pages=n_pages, page=page, lam_init=lam_init),
        grid_spec=grid_spec,
        out_shape=jax.ShapeDtypeStruct((nb, 1, DV_W), f32),
        compiler_params=_cparams(("arbitrary",)),
        name="attn_sample",
    )(page_table.reshape(-1), q.reshape(nb, 1, -1), k_new.reshape(nb, 1, -1), v_new.reshape(nb, 1, -1), lqk,
      cache_k, cache_v)
    return out.reshape(nb, DV_W)


def _gla_sample_kernel(q_ref, k_ref, v_ref, g_ref, s0_ref, o_ref, s_ref, *, tb):
    ri = lax.broadcasted_iota(jnp.int32, (GLA_DK, GLA_DK), 0)
    ci = lax.broadcasted_iota(jnp.int32, (GLA_DK, GLA_DK), 1)
    eye = ri == ci

    def column(r):
        return jnp.sum(jnp.where(eye, jnp.broadcast_to(r, (GLA_DK, GLA_DK)), 0.0), axis=-1, keepdims=True)

    q = q_ref[...]
    k = k_ref[...]
    v = v_ref[...]
    e = jnp.exp(g_ref[...])
    for i in range(tb):
        for h in range(GLA_HEADS):
            ks = slice(h * GLA_DK, (h + 1) * GLA_DK)
            vs = slice(h * GLA_DV, (h + 1) * GLA_DV)
            s_new = s0_ref[i, h] * column(e[i:i + 1, ks]) + column(k[i:i + 1, ks]) * v[i:i + 1, vs]
            s_ref[i, h] = s_new
            o_ref[i:i + 1, vs] = jnp.sum(column(q[i:i + 1, ks]) * s_new, axis=0, keepdims=True)


def _gla_sample(gq, gk, gv, gl, s0, *, tb):
    nb = gq.shape[0]
    row = lambda w: pl.BlockSpec((tb, w), lambda i: (i, 0))
    st = pl.BlockSpec((tb, GLA_HEADS, GLA_DK, GLA_DV), lambda i: (i, 0, 0, 0))
    return pl.pallas_call(
        functools.partial(_gla_sample_kernel, tb=tb),
        grid=(nb // tb,),
        in_specs=[row(GK_W), row(GK_W), row(GV_W), row(GK_W), st],
        out_specs=(row(GV_W), st),
        out_shape=(jax.ShapeDtypeStruct((nb, GV_W), f32), jax.ShapeDtypeStruct(s0.shape, f32)),
        compiler_params=_cparams(("parallel",)),
        name="gla_sample",
    )(gq, gk, gv, gl, s0)


def _rope_tables(pos):
    half = DIFF_DK // 2
    freqs = ROPE_THETA ** (-jnp.arange(half, dtype=f32) / half)
    ang = pos.astype(f32)[:, None] * freqs[None, :]
    cos, sin = jnp.cos(ang), jnp.sin(ang)
    reps = LANES // DIFF_DK
    return jnp.tile(jnp.concatenate([cos, cos], -1), (1, reps)), jnp.tile(jnp.concatenate([-sin, sin], -1), (1, reps))


def kernel(x_prompt, x_sample, cache_k, cache_v, state_gla, page_table, norm1, w_in, q_norm, k_norm, lambda_qk,
           subln, w_a2, b_a, gla_norm, w_out, norm2, w_gu, w_down):
    batch, seq, d = x_prompt.shape
    nb, dec_seq, _ = x_sample.shape
    assert dec_seq == 1
    depth = w_in.shape[0]
    n_pages = page_table.shape[1]
    page = cache_k.shape[2]
    past = n_pages * page
    rank = w_a2.shape[1]

    cos_p, sin_p = _rope_tables(jnp.arange(seq))
    cos_s, sin_s = _rope_tables(past + jnp.arange(dec_seq))
    cos_s = jnp.broadcast_to(cos_s, (nb, LANES))
    sin_s = jnp.broadcast_to(sin_s, (nb, LANES))

    ck = cache_k.reshape(depth, -1, page, QK_W)
    cv = cache_v.reshape(depth, -1, page, DV_W)

    yp = x_prompt.reshape(batch * seq, d)
    ys = x_sample.reshape(nb, d)
    kp, vp, sp, k_s, v_s, s_s = [], [], [], [], [], []
    for l in range(depth):
        lam_init = 0.8 - 0.6 * math.exp(-0.3 * l)
        w_main = w_in[l, :, :MAIN_W].astype(bf16)
        w_r = jnp.pad(w_in[l, :, MAIN_W:], ((0, 0), (0, LANES - rank))).astype(bf16)
        wa2 = jnp.pad(w_a2[l], ((0, LANES - rank), (0, 0))).astype(bf16)
        proj_args = (norm1[l][None], w_main, w_r, wa2, b_a[l][None],
                     jnp.tile(q_norm[l], QK_W // DIFF_DK)[None], jnp.tile(k_norm[l], QK_W // DIFF_DK)[None])
        post_args = (jnp.tile(subln[l], DIFF_HEADS)[None], jnp.tile(gla_norm[l], GLA_HEADS)[None],
                     w_out[l].astype(bf16), norm2[l][None], w_gu[l].astype(bf16), w_down[l].astype(bf16))

        q, k, v, gq, gk, gv, gl, gg = _proj(yp, *proj_args, cos_p, sin_p, tm=256, narrow=True)
        od = _attn_prompt(q, k, v, lambda_qk[l], batch=batch, seq=seq, tq=256, tk=256, lam_init=lam_init)
        og, s_fin = _gla_prompt(gq, gk, gv, gl, batch=batch, seq=seq, tg=512)
        yp = _post(yp, od, og, gg, *post_args, tm=256, lam_init=lam_init)
        kp.append(k.reshape(batch, seq, DIFF_HEADS, 2, DIFF_DK))
        vp.append(v.reshape(batch, seq, DIFF_HEADS, DIFF_DV))
        sp.append(s_fin)

        q, k, v, gq, gk, gv, gl, gg = _proj(ys, *proj_args, cos_s, sin_s, tm=nb, narrow=False)
        od = _attn_sample(page_table, q, k, v, lambda_qk[l], ck, cv, layer=l, lam_init=lam_init)
        og, s_new = _gla_sample(gq, gk, gv, gl, state_gla[l], tb=8)
        ys = _post(ys, od, og, gg, *post_args, tm=nb, lam_init=lam_init)
        k_s.append(k.reshape(nb, dec_seq, DIFF_HEADS, 2, DIFF_DK))
        v_s.append(v.reshape(nb, dec_seq, DIFF_HEADS, DIFF_DV))
        s_s.append(s_new)

    return (yp.reshape(batch, seq, d), ys.reshape(nb, dec_seq, d), jnp.stack(kp), jnp.stack(vp), jnp.stack(sp),
            jnp.stack(k_s), jnp.stack(v_s), jnp.stack(s_s))
```

```python
import functools
import math

import jax
import jax.numpy as jnp
from jax import lax
from jax.experimental import pallas as pl
from jax.experimental.pallas import tpu as pltpu

f32 = jnp.float32
bf16 = jnp.bfloat16

DIFF_HEADS = 4
DIFF_DK = 64
DIFF_DV = 128
GLA_HEADS = 4
GLA_DK = 64
GLA_DV = 128
GLA_GATE_NORM = 16.0
GLA_CHUNK = 64
ROPE_THETA = 10000.0
EPS = 1e-6
LOG2_E = math.log2(math.e)

QK_W = DIFF_HEADS * 2 * DIFF_DK
DV_W = DIFF_HEADS * DIFF_DV
GK_W = GLA_HEADS * GLA_DK
GV_W = GLA_HEADS * GLA_DV
MAIN_W = 2 * QK_W + DV_W + 2 * GK_W + 2 * GV_W

LANES = 128
VMEM_LIMIT = 56 * 1024 * 1024


def _cparams(sem):
    return pltpu.CompilerParams(dimension_semantics=sem, vmem_limit_bytes=VMEM_LIMIT)


def _const_spec(shape):
    nd = len(shape)
    return pl.BlockSpec(shape, lambda *_: (0,) * nd, pipeline_mode=pl.Buffered(1))


def _group_scale(z, width):
    lane = lax.broadcasted_iota(jnp.int32, (z.shape[0], LANES), 1)
    cols = []
    for c in range(z.shape[1] // LANES):
        zc = z[:, c * LANES:(c + 1) * LANES]
        zz = zc * zc
        if width == LANES:
            cols.append(jnp.broadcast_to(lax.rsqrt(jnp.mean(zz, axis=-1, keepdims=True) + EPS), zc.shape))
        else:
            lo = lane < width
            s_lo = jnp.sum(jnp.where(lo, zz, 0.0), axis=-1, keepdims=True)
            s_hi = jnp.sum(jnp.where(lo, 0.0, zz), axis=-1, keepdims=True)
            r_lo = lax.rsqrt(s_lo * (1.0 / width) + EPS)
            r_hi = lax.rsqrt(s_hi * (1.0 / width) + EPS)
            cols.append(jnp.where(lo, r_lo, r_hi))
    return jnp.concatenate(cols, axis=-1)


def _rope(z, cos, sin_signed):
    lane = lax.broadcasted_iota(jnp.int32, (z.shape[0], LANES), 1)
    first_half = (lane % DIFF_DK) < (DIFF_DK // 2)
    cols = []
    for c in range(z.shape[1] // LANES):
        zc = z[:, c * LANES:(c + 1) * LANES]
        partner = jnp.where(first_half, pltpu.roll(zc, LANES - DIFF_DK // 2, 1), pltpu.roll(zc, DIFF_DK // 2, 1))
        cols.append(zc * cos + partner * sin_signed)
    return jnp.concatenate(cols, axis=-1)


def _diff_lambda(lqk, lam_init):
    a = jnp.sum(lqk[0:1, :] * lqk[1:2, :], axis=-1, keepdims=True)
    b = jnp.sum(lqk[2:3, :] * lqk[3:4, :], axis=-1, keepdims=True)
    return jnp.exp(a) - jnp.exp(b) + lam_init


def _proj_kernel(x_ref, g1_ref, w_ref, wr_ref, wa2_ref, ba_ref, qn_ref, kn_ref, cos_ref, sin_ref,
                 q_ref, k_ref, v_ref, gq_ref, gk_ref, gv_ref, gl_ref, gg_ref, *, split_q):
    x = x_ref[...]
    n = x * lax.rsqrt(jnp.mean(x * x, axis=-1, keepdims=True) + EPS) * g1_ref[...]
    nb = n.astype(bf16)

    def seg(lo, width):
        return jnp.dot(nb, w_ref[:, lo:lo + width], preferred_element_type=f32)

    cos = cos_ref[...]
    sin = sin_ref[...]
    lane = lax.broadcasted_iota(jnp.int32, (x.shape[0], LANES), 1)
    first_map = lane < DIFF_DK

    zq = seg(0, QK_W)
    q = _rope(zq * _group_scale(zq, DIFF_DK) * qn_ref[...], cos, sin) * (DIFF_DK ** -0.5 * LOG2_E)
    if split_q:
        for h in range(DIFF_HEADS):
            qh = q[:, h * LANES:(h + 1) * LANES]
            q_ref[:, (2 * h) * LANES:(2 * h + 1) * LANES] = jnp.where(first_map, qh, 0.0).astype(q_ref.dtype)
            q_ref[:, (2 * h + 1) * LANES:(2 * h + 2) * LANES] = jnp.where(first_map, 0.0, qh).astype(q_ref.dtype)
    else:
        q_ref[...] = q

    zk = seg(QK_W, QK_W)
    k_ref[...] = _rope(zk * _group_scale(zk, DIFF_DK) * kn_ref[...], cos, sin)
    v_ref[...] = seg(2 * QK_W, DV_W)

    off = 2 * QK_W + DV_W
    gq_ref[...] = seg(off, GK_W) * (GLA_DK ** -0.5)
    gk_ref[...] = seg(off + GK_W, GK_W)
    gv_ref[...] = seg(off + 2 * GK_W, GV_W).astype(gv_ref.dtype)
    gg_ref[...] = seg(off + 2 * GK_W + GV_W, GV_W)

    r = jnp.dot(nb, wr_ref[...], preferred_element_type=f32)
    a = jnp.dot(r.astype(bf16), wa2_ref[...], preferred_element_type=f32) + ba_ref[...]
    gl_ref[...] = (jnp.minimum(a, 0.0) - jnp.log1p(jnp.exp(-jnp.abs(a)))) * (1.0 / GLA_GATE_NORM)


def _proj(x2d, g1, w_main, w_r, w_a2, b_a, qn, kn, cos_t, sin_t, *, tm, prompt):
    n, d = x2d.shape
    nt = cos_t.shape[0] // tm
    row = lambda w: pl.BlockSpec((tm, w), lambda i: (i, 0))
    tab = pl.BlockSpec((tm, LANES), lambda i: (i % nt, 0))
    qdt = bf16 if prompt else f32
    q_w = 2 * QK_W if prompt else QK_W
    out_shape = (
        jax.ShapeDtypeStruct((n, q_w), qdt),
        jax.ShapeDtypeStruct((n, QK_W), f32),
        jax.ShapeDtypeStruct((n, DV_W), f32),
        jax.ShapeDtypeStruct((n, GK_W), f32),
        jax.ShapeDtypeStruct((n, GK_W), f32),
        jax.ShapeDtypeStruct((n, GV_W), qdt),
        jax.ShapeDtypeStruct((n, GK_W), f32),
        jax.ShapeDtypeStruct((n, GV_W), f32),
    )
    return pl.pallas_call(
        functools.partial(_proj_kernel, split_q=prompt),
        grid=(n // tm,),
        in_specs=[row(d), _const_spec((1, d)), _const_spec(w_main.shape), _const_spec(w_r.shape),
                  _const_spec(w_a2.shape), _const_spec((1, GK_W)), _const_spec((1, QK_W)), _const_spec((1, QK_W)),
                  tab, tab],
        out_specs=(row(q_w), row(QK_W), row(DV_W), row(GK_W), row(GK_W), row(GV_W), row(GK_W), row(GV_W)),
        out_shape=out_shape,
        compiler_params=_cparams(("parallel",)),
        name="proj",
    )(x2d, g1, w_main, w_r, w_a2, b_a, qn, kn, cos_t, sin_t)


def _attn_kernel(q_ref, k_ref, v_ref, lqk_ref, o_ref, kb_ref, vb_ref, s_ref, acc_ref, *, tq, tk, lam_init):
    qi = pl.program_id(2)

    @pl.when(qi == 0)
    def _():
        kb_ref[...] = k_ref[...].astype(bf16)
        vb_ref[:, :DIFF_DV] = v_ref[...].astype(bf16)
        vb_ref[:, DIFF_DV:] = jnp.ones((vb_ref.shape[0], LANES), bf16)

    qs = jnp.concatenate([q_ref[:, :LANES], q_ref[:, LANES:]], axis=0)
    nfull = qi * (tq // tk)
    ndiag = tq // tk

    def scores(kt, diag_j):
        r0 = pl.multiple_of(kt * tk, tk)
        s = lax.dot_general(qs, kb_ref[pl.ds(r0, tk), :], (((1,), (1,)), ((), ())), preferred_element_type=f32)
        if diag_j is not None:
            row = lax.broadcasted_iota(jnp.int32, (2 * tq, tk), 0) % tq
            col = lax.broadcasted_iota(jnp.int32, (2 * tq, tk), 1) + diag_j * tk
            s = jnp.where(col <= row, s, -jnp.inf)
        s_ref[:, pl.ds(r0, tk)] = s
        mx = s[:, :LANES]
        for c in range(1, tk // LANES):
            mx = jnp.maximum(mx, s[:, c * LANES:(c + 1) * LANES])
        return mx

    mx = jnp.full((2 * tq, LANES), -jnp.inf, f32)
    mx = lax.fori_loop(0, nfull, lambda kt, m: jnp.maximum(m, scores(kt, None)), mx)
    for j in range(ndiag):
        mx = jnp.maximum(mx, scores(nfull + j, j))
    m = jnp.broadcast_to(jnp.max(mx, axis=-1, keepdims=True), (2 * tq, tk))

    acc_ref[...] = jnp.zeros(acc_ref.shape, f32)

    def accumulate(kt, carry):
        r0 = pl.multiple_of(kt * tk, tk)
        p = jnp.exp2(s_ref[:, pl.ds(r0, tk)] - m).astype(bf16)
        acc_ref[...] += jnp.dot(p, vb_ref[pl.ds(r0, tk), :], preferred_element_type=f32)
        return carry

    lax.fori_loop(0, nfull + ndiag, accumulate, 0)

    acc = acc_ref[...]
    lam = _diff_lambda(lqk_ref[...], lam_init)
    o1 = acc[:tq, :DIFF_DV] / acc[:tq, DIFF_DV:]
    o2 = acc[tq:, :DIFF_DV] / acc[tq:, DIFF_DV:]
    o_ref[...] = o1 - lam * o2


def _attn_prompt(q, k, v, lqk, *, batch, seq, tq, tk, lam_init):
    nq = seq // tq
    return pl.pallas_call(
        functools.partial(_attn_kernel, tq=tq, tk=tk, lam_init=lam_init),
        grid=(batch, DIFF_HEADS, nq),
        in_specs=[pl.BlockSpec((tq, 2 * LANES), lambda b, h, i: (b * nq + i, h)),
                  pl.BlockSpec((seq, LANES), lambda b, h, i: (b, h)),
                  pl.BlockSpec((seq, LANES), lambda b, h, i: (b, h)),
                  _const_spec(lqk.shape)],
        out_specs=pl.BlockSpec((tq, DIFF_DV), lambda b, h, i: (b * nq + i, h)),
        out_shape=jax.ShapeDtypeStruct((batch * seq, DV_W), f32),
        scratch_shapes=[pltpu.VMEM((seq, LANES), bf16), pltpu.VMEM((seq, 2 * LANES), bf16),
                        pltpu.VMEM((2 * tq, seq), f32), pltpu.VMEM((2 * tq, 2 * LANES), f32)],
        compiler_params=_cparams(("parallel", "parallel", "arbitrary")),
        name="attn_prompt",
    )(q, k, v, lqk)


def _cumsum_rows(g, tril):
    g1 = g.astype(bf16)
    r1 = g - g1.astype(f32)
    g2 = r1.astype(bf16)
    g3 = (r1 - g2.astype(f32)).astype(bf16)
    dot = lambda t: jnp.dot(tril, t, preferred_element_type=f32)
    return dot(g1) + dot(g2) + dot(g3)


def _gla_kernel(q_ref, k_ref, v_ref, g_ref, o_ref, s_ref, st_ref, *, tg):
    step = pl.program_id(1)

    @pl.when(step == 0)
    def _():
        st_ref[...] = jnp.zeros(st_ref.shape, f32)

    c = GLA_CHUNK
    ri = lax.broadcasted_iota(jnp.int32, (c, c), 0)
    ci = lax.broadcasted_iota(jnp.int32, (c, c), 1)
    causal = ri >= ci
    tril = jnp.where(causal, 1.0, 0.0).astype(bf16)
    lane = lax.broadcasted_iota(jnp.int32, (c, LANES), 1)
    head0 = lane < GLA_DK
    lane_sq = lax.broadcasted_iota(jnp.int32, (LANES, LANES), 1) < GLA_DK

    def chunk(ic, carry):
        r0 = pl.multiple_of(ic * c, c)
        rows = pl.ds(r0, c)
        b = _cumsum_rows(g_ref[rows, :], tril)
        q = q_ref[rows, :]
        k = k_ref[rows, :]
        b_mid = b[c // 2 - 1:c // 2, :]
        b_last = b[c - 1:c, :]
        qe = q * jnp.exp(b - b_mid)
        ke = (k * jnp.exp(b_mid - b)).astype(bf16)
        qb = q * jnp.exp(b)
        kd = (k * jnp.exp(b_last - b)).astype(bf16)
        e_last = jnp.exp(b_last)
        for p in range(GLA_HEADS // 2):
            sl = slice(p * LANES, (p + 1) * LANES)
            st = st_ref[p]
            stb = st.astype(bf16)
            upd = []
            for j in range(2):
                h = 2 * p + j
                mine = head0 if j == 0 else jnp.logical_not(head0)
                qe_h = jnp.where(mine, qe[:, sl], 0.0).astype(bf16)
                qb_h = jnp.where(mine, qb[:, sl], 0.0).astype(bf16)
                v_h = v_ref[rows, h * GLA_DV:(h + 1) * GLA_DV]
                a = lax.dot_general(qe_h, ke[:, sl], (((1,), (1,)), ((), ())), preferred_element_type=f32)
                a = jnp.where(causal, a, 0.0).astype(bf16)
                o = jnp.dot(a, v_h, preferred_element_type=f32)
                o = o + lax.dot_general(qb_h, stb, (((1,), (1,)), ((), ())), preferred_element_type=f32)
                o_ref[rows, h * GLA_DV:(h + 1) * GLA_DV] = o
                upd.append(lax.dot_general(v_h, kd[:, sl], (((0,), (0,)), ((), ())), preferred_element_type=f32))
            st_ref[p] = st * e_last[:, sl] + jnp.where(lane_sq, upd[0], upd[1])
        return carry

    lax.fori_loop(0, tg // c, chunk, 0)

    @pl.when(step == pl.num_programs(1) - 1)
    def _():
        for p in range(GLA_HEADS // 2):
            t = st_ref[p].T
            s_ref[0, 2 * p] = t[:GLA_DK, :]
            s_ref[0, 2 * p + 1] = t[GLA_DK:, :]


def _gla_prompt(gq, gk, gv, gl, *, batch, seq, tg):
    ns = seq // tg
    row = lambda w: pl.BlockSpec((tg, w), lambda b, i: (b * ns + i, 0))
    return pl.pallas_call(
        functools.partial(_gla_kernel, tg=tg),
        grid=(batch, ns),
        in_specs=[row(GK_W), row(GK_W), row(GV_W), row(GK_W)],
        out_specs=(row(GV_W), pl.BlockSpec((1, GLA_HEADS, GLA_DK, GLA_DV), lambda b, i: (b, 0, 0, 0))),
        out_shape=(jax.ShapeDtypeStruct((batch * seq, GV_W), f32),
                   jax.ShapeDtypeStruct((batch, GLA_HEADS, GLA_DK, GLA_DV), f32)),
        scratch_shapes=[pltpu.VMEM((GLA_HEADS // 2, LANES, LANES), f32)],
        compiler_params=_cparams(("parallel", "arbitrary")),
        name="gla_prompt",
    )(gq, gk, gv, gl)


def _post_kernel(x_ref, od_ref, og_ref, gg_ref, sub_ref, gn_ref, wo_ref, n2_ref, wgu_ref, wd_ref, y_ref,
                 *, lam_init, d_ff, ff_chunk):
    od = od_ref[...]
    odn = od * _group_scale(od, DIFF_DV) * sub_ref[...] * (1.0 - lam_init)
    og = og_ref[...]
    ogn = og * _group_scale(og, GLA_DV) * gn_ref[...]
    gg = gg_ref[...]
    ogn = ogn * (gg * jax.nn.sigmoid(gg))
    mix = jnp.concatenate([odn, ogn], axis=-1).astype(bf16)
    y = x_ref[...] + jnp.dot(mix, wo_ref[...], preferred_element_type=f32)

    n2 = (y * lax.rsqrt(jnp.mean(y * y, axis=-1, keepdims=True) + EPS) * n2_ref[...]).astype(bf16)
    acc = y
    for c0 in range(0, d_ff, ff_chunk):
        a = jnp.dot(n2, wgu_ref[:, c0:c0 + ff_chunk], preferred_element_type=f32)
        b = jnp.dot(n2, wgu_ref[:, d_ff + c0:d_ff + c0 + ff_chunk], preferred_element_type=f32)
        hid = (a * jax.nn.sigmoid(a) * b).astype(bf16)
        acc = acc + jnp.dot(hid, wd_ref[c0:c0 + ff_chunk, :], preferred_element_type=f32)
    y_ref[...] = acc


def _post(x2d, od, og, gg, sub, gn, w_out, n2, w_gu, w_down, *, tm, lam_init):
    n, d = x2d.shape
    d_ff = w_down.shape[0]
    row = lambda w: pl.BlockSpec((tm, w), lambda i: (i, 0))
    return pl.pallas_call(
        functools.partial(_post_kernel, lam_init=lam_init, d_ff=d_ff, ff_chunk=2 * LANES),
        grid=(n // tm,),
        in_specs=[row(d), row(DV_W), row(GV_W), row(GV_W), _const_spec((1, DV_W)), _const_spec((1, GV_W)),
                  _const_spec(w_out.shape), _const_spec((1, d)), _const_spec(w_gu.shape), _const_spec(w_down.shape)],
        out_specs=row(d),
        out_shape=jax.ShapeDtypeStruct((n, d), f32),
        compiler_params=_cparams(("parallel",)),
        name="post",
    )(x2d, od, og, gg, sub, gn, w_out, n2, w_gu, w_down)


def _attn_sample_kernel(pt_ref, q_ref, kn_ref, vn_ref, lqk_ref, ck_hbm, cv_hbm, o_ref, kbuf, vbuf, sem,
                        *, layer, n_pages, page, lam_init):
    b = pl.program_id(0)
    nb = pl.num_programs(0)

    def copies(sample, slot):
        out = []
        for j in range(n_pages):
            phys = pt_ref[sample * n_pages + j]
            out.append(pltpu.make_async_copy(ck_hbm.at[layer, phys], kbuf.at[slot, j], sem.at[0, slot]))
            out.append(pltpu.make_async_copy(cv_hbm.at[layer, phys], vbuf.at[slot, j], sem.at[1, slot]))
        return out

    @pl.when(b == 0)
    def _():
        for cp in copies(0, 0):
            cp.start()

    slot = b % 2

    @pl.when(b + 1 < nb)
    def _():
        for cp in copies(b + 1, 1 - slot):
            cp.start()

    for cp in copies(b, slot):
        cp.wait()

    past = n_pages * page
    nrow = 2 * DIFF_HEADS
    q = q_ref[0]
    qb = q.astype(bf16)
    rid = lax.broadcasted_iota(jnp.int32, (nrow, past), 0)
    s = jnp.zeros((nrow, past), f32)
    for h in range(DIFF_HEADS):
        for mp in range(2):
            kt = jnp.concatenate([kbuf[slot, j, h, mp] for j in range(n_pages)], axis=-1)
            sr = jnp.dot(qb, kt.astype(bf16), preferred_element_type=f32)
            s = jnp.where(rid == 2 * h + mp, sr, s)
    s_new = jnp.sum(q * kn_ref[0], axis=-1, keepdims=True)
    m = jnp.maximum(jnp.max(s, axis=-1, keepdims=True), s_new)
    p = jnp.exp2(s - m)
    p_new = jnp.exp2(s_new - m)
    inv_l = 1.0 / (jnp.sum(p, axis=-1, keepdims=True) + p_new)
    pb = p.astype(bf16)
    lam = _diff_lambda(lqk_ref[...], lam_init)
    vn = vn_ref[0]
    heads = []
    for h in range(DIFF_HEADS):
        vh = jnp.concatenate([vbuf[slot, j, pl.ds(h, page, stride=DIFF_HEADS), :] for j in range(n_pages)], axis=0)
        oh = jnp.dot(pb, vh.astype(bf16), preferred_element_type=f32)
        oh = (oh + p_new * vn[:, h * DIFF_DV:(h + 1) * DIFF_DV]) * inv_l
        heads.append(oh[2 * h:2 * h + 1, :] - lam * oh[2 * h + 1:2 * h + 2, :])
    o_ref[0] = jnp.concatenate(heads, axis=-1)


def _attn_sample(page_table, q, k_new, v_new, lqk, cache_kt, cache_vr, *, layer, lam_init):
    nb, n_pages = page_table.shape
    page = cache_kt.shape[-1]
    nrow = 2 * DIFF_HEADS
    grid_spec = pltpu.PrefetchScalarGridSpec(
        num_scalar_prefetch=1,
        grid=(nb,),
        in_specs=[pl.BlockSpec((1, nrow, DIFF_DK), lambda b, pt: (b, 0, 0)),
                  pl.BlockSpec((1, nrow, DIFF_DK), lambda b, pt: (b, 0, 0)),
                  pl.BlockSpec((1, 1, DV_W), lambda b, pt: (b, 0, 0)),
                  pl.BlockSpec(lqk.shape, lambda b, pt: (0, 0)),
                  pl.BlockSpec(memory_space=pl.ANY), pl.BlockSpec(memory_space=pl.ANY)],
        out_specs=pl.BlockSpec((1, 1, DV_W), lambda b, pt: (b, 0, 0)),
        scratch_shapes=[pltpu.VMEM((2, n_pages) + cache_kt.shape[2:], f32),
                        pltpu.VMEM((2, n_pages) + cache_vr.shape[2:], f32),
                        pltpu.SemaphoreType.DMA((2, 2))],
    )
    out = pl.pallas_call(
        functools.partial(_attn_sample_kernel, layer=layer, n_pages=n_pages, page=page, lam_init=lam_init),
        grid_spec=grid_spec,
        out_shape=jax.ShapeDtypeStruct((nb, 1, DV_W), f32),
        compiler_params=_cparams(("arbitrary",)),
        name="attn_sample",
    )(page_table.reshape(-1), q.reshape(nb, nrow, DIFF_DK), k_new.reshape(nb, nrow, DIFF_DK),
      v_new.reshape(nb, 1, DV_W), lqk, cache_kt, cache_vr)
    return out.reshape(nb, DV_W)


def _gla_sample_kernel(q_ref, k_ref, v_ref, g_ref, s0_ref, o_ref, s_ref, *, tb):
    ri = lax.broadcasted_iota(jnp.int32, (GLA_DK, GLA_DK), 0)
    ci = lax.broadcasted_iota(jnp.int32, (GLA_DK, GLA_DK), 1)
    eye = ri == ci

    def column(r):
        return jnp.sum(jnp.where(eye, jnp.broadcast_to(r, (GLA_DK, GLA_DK)), 0.0), axis=-1, keepdims=True)

    q = q_ref[...]
    k = k_ref[...]
    v = v_ref[...]
    e = jnp.exp(g_ref[...])
    for i in range(tb):
        for h in range(GLA_HEADS):
            ks = slice(h * GLA_DK, (h + 1) * GLA_DK)
            vs = slice(h * GLA_DV, (h + 1) * GLA_DV)
            s_new = s0_ref[i, h] * column(e[i:i + 1, ks]) + column(k[i:i + 1, ks]) * v[i:i + 1, vs]
            s_ref[i, h] = s_new
            o_ref[i:i + 1, vs] = jnp.sum(column(q[i:i + 1, ks]) * s_new, axis=0, keepdims=True)


def _gla_sample(gq, gk, gv, gl, s0, *, tb):
    nb = gq.shape[0]
    row = lambda w: pl.BlockSpec((tb, w), lambda i: (i, 0))
    st = pl.BlockSpec((tb, GLA_HEADS, GLA_DK, GLA_DV), lambda i: (i, 0, 0, 0))
    return pl.pallas_call(
        functools.partial(_gla_sample_kernel, tb=tb),
        grid=(nb // tb,),
        in_specs=[row(GK_W), row(GK_W), row(GV_W), row(GK_W), st],
        out_specs=(row(GV_W), st),
        out_shape=(jax.ShapeDtypeStruct((nb, GV_W), f32), jax.ShapeDtypeStruct(s0.shape, f32)),
        compiler_params=_cparams(("parallel",)),
        name="gla_sample",
    )(gq, gk, gv, gl, s0)


def _rope_tables(pos):
    half = DIFF_DK // 2
    freqs = ROPE_THETA ** (-jnp.arange(half, dtype=f32) / half)
    ang = pos.astype(f32)[:, None] * freqs[None, :]
    cos, sin = jnp.cos(ang), jnp.sin(ang)
    reps = LANES // DIFF_DK
    return jnp.tile(jnp.concatenate([cos, cos], -1), (1, reps)), jnp.tile(jnp.concatenate([-sin, sin], -1), (1, reps))


def kernel(x_prompt, x_sample, cache_k, cache_v, state_gla, page_table, norm1, w_in, q_norm, k_norm, lambda_qk,
           subln, w_a2, b_a, gla_norm, w_out, norm2, w_gu, w_down):
    batch, seq, d = x_prompt.shape
    nb, dec_seq, _ = x_sample.shape
    assert dec_seq == 1
    depth = w_in.shape[0]
    n_pages = page_table.shape[1]
    page = cache_k.shape[2]
    past = n_pages * page
    rank = w_a2.shape[1]

    cos_p, sin_p = _rope_tables(jnp.arange(seq))
    cos_s, sin_s = _rope_tables(past + jnp.arange(dec_seq))
    cos_s = jnp.broadcast_to(cos_s, (nb, LANES))
    sin_s = jnp.broadcast_to(sin_s, (nb, LANES))

    ck = jnp.transpose(cache_k, (0, 1, 3, 4, 5, 2))
    cv = cache_v.reshape(depth, -1, page * DIFF_HEADS, DIFF_DV)

    yp = x_prompt.reshape(batch * seq, d)
    ys = x_sample.reshape(nb, d)
    kp, vp, sp, k_s, v_s, s_s = [], [], [], [], [], []
    for l in range(depth):
        lam_init = 0.8 - 0.6 * math.exp(-0.3 * l)
        w_main = w_in[l, :, :MAIN_W].astype(bf16)
        w_r = jnp.pad(w_in[l, :, MAIN_W:], ((0, 0), (0, LANES - rank))).astype(bf16)
        wa2 = jnp.pad(w_a2[l], ((0, LANES - rank), (0, 0))).astype(bf16)
        proj_args = (norm1[l][None], w_main, w_r, wa2, b_a[l][None],
                     jnp.tile(q_norm[l], QK_W // DIFF_DK)[None], jnp.tile(k_norm[l], QK_W // DIFF_DK)[None])
        post_args = (jnp.tile(subln[l], DIFF_HEADS)[None], jnp.tile(gla_norm[l], GLA_HEADS)[None],
                     w_out[l].astype(bf16), norm2[l][None], w_gu[l].astype(bf16), w_down[l].astype(bf16))

        q, k, v, gq, gk, gv, gl, gg = _proj(yp, *proj_args, cos_p, sin_p, tm=256, prompt=True)
        od = _attn_prompt(q, k, v, lambda_qk[l], batch=batch, seq=seq, tq=512, tk=512, lam_init=lam_init)
        og, s_fin = _gla_prompt(gq, gk, gv, gl, batch=batch, seq=seq, tg=512)
        yp = _post(yp, od, og, gg, *post_args, tm=256, lam_init=lam_init)
        kp.append(k.reshape(batch, seq, DIFF_HEADS, 2, DIFF_DK))
        vp.append(v.reshape(batch, seq, DIFF_HEADS, DIFF_DV))
        sp.append(s_fin)

        q, k, v, gq, gk, gv, gl, gg = _proj(ys, *proj_args, cos_s, sin_s, tm=nb, prompt=False)
        od = _attn_sample(page_table, q, k, v, lambda_qk[l], ck, cv, layer=l, lam_init=lam_init)
        og, s_new = _gla_sample(gq, gk, gv, gl, state_gla[l], tb=8)
        ys = _post(ys, od, og, gg, *post_args, tm=nb, lam_init=lam_init)
        k_s.append(k.reshape(nb, dec_seq, DIFF_HEADS, 2, DIFF_DK))
        v_s.append(v.reshape(nb, dec_seq, DIFF_HEADS, DIFF_DV))
        s_s.append(s_new)

    return (yp.reshape(batch, seq, d), ys.reshape(nb, dec_seq, d), jnp.stack(kp), jnp.stack(vp), jnp.stack(sp),
            jnp.stack(k_s), jnp.stack(v_s), jnp.stack(s_s))
```

```python
import functools
import math

import jax
import jax.numpy as jnp
from jax import lax
from jax.experimental import pallas as pl
from jax.experimental.pallas import tpu as pltpu

f32 = jnp.float32
bf16 = jnp.bfloat16

DIFF_HEADS = 4
DIFF_DK = 64
DIFF_DV = 128
GLA_HEADS = 4
GLA_DK = 64
GLA_DV = 128
GLA_GATE_NORM = 16.0
GLA_CHUNK = 64
ROPE_THETA = 10000.0
EPS = 1e-6
LOG2_E = math.log2(math.e)

QK_W = DIFF_HEADS * 2 * DIFF_DK
DV_W = DIFF_HEADS * DIFF_DV
GK_W = GLA_HEADS * GLA_DK
GV_W = GLA_HEADS * GLA_DV
MAIN_W = 2 * QK_W + DV_W + 2 * GK_W + 2 * GV_W

LANES = 128
GLA_SAMPLE_BLOCK = 32
VMEM_LIMIT = 56 * 1024 * 1024


def _cparams(sem):
    return pltpu.CompilerParams(dimension_semantics=sem, vmem_limit_bytes=VMEM_LIMIT)


def _const_spec(shape):
    nd = len(shape)
    return pl.BlockSpec(shape, lambda *_: (0,) * nd, pipeline_mode=pl.Buffered(1))


def _group_scale(z, width):
    lane = lax.broadcasted_iota(jnp.int32, (z.shape[0], LANES), 1)
    cols = []
    for c in range(z.shape[1] // LANES):
        zc = z[:, c * LANES:(c + 1) * LANES]
        zz = zc * zc
        if width == LANES:
            cols.append(jnp.broadcast_to(lax.rsqrt(jnp.mean(zz, axis=-1, keepdims=True) + EPS), zc.shape))
        else:
            lo = lane < width
            s_lo = jnp.sum(jnp.where(lo, zz, 0.0), axis=-1, keepdims=True)
            s_hi = jnp.sum(jnp.where(lo, 0.0, zz), axis=-1, keepdims=True)
            r_lo = lax.rsqrt(s_lo * (1.0 / width) + EPS)
            r_hi = lax.rsqrt(s_hi * (1.0 / width) + EPS)
            cols.append(jnp.where(lo, r_lo, r_hi))
    return jnp.concatenate(cols, axis=-1)


def _rope(z, cos, sin_signed):
    lane = lax.broadcasted_iota(jnp.int32, (z.shape[0], LANES), 1)
    first_half = (lane % DIFF_DK) < (DIFF_DK // 2)
    cols = []
    for c in range(z.shape[1] // LANES):
        zc = z[:, c * LANES:(c + 1) * LANES]
        partner = jnp.where(first_half, pltpu.roll(zc, LANES - DIFF_DK // 2, 1), pltpu.roll(zc, DIFF_DK // 2, 1))
        cols.append(zc * cos + partner * sin_signed)
    return jnp.concatenate(cols, axis=-1)


def _diff_lambda(lqk, lam_init):
    a = jnp.sum(lqk[0:1, :] * lqk[1:2, :], axis=-1, keepdims=True)
    b = jnp.sum(lqk[2:3, :] * lqk[3:4, :], axis=-1, keepdims=True)
    return jnp.exp(a) - jnp.exp(b) + lam_init


def _proj_body(x, g1_ref, w_ref, wr_ref, wa2_ref, ba_ref, qn_ref, kn_ref, cos, sin):
    n = x * lax.rsqrt(jnp.mean(x * x, axis=-1, keepdims=True) + EPS) * g1_ref[...]
    nb = n.astype(bf16)

    def seg(lo, width):
        return jnp.dot(nb, w_ref[:, lo:lo + width], preferred_element_type=f32)

    zq = seg(0, QK_W)
    q = _rope(zq * _group_scale(zq, DIFF_DK) * qn_ref[...], cos, sin) * (DIFF_DK ** -0.5 * LOG2_E)
    zk = seg(QK_W, QK_W)
    k = _rope(zk * _group_scale(zk, DIFF_DK) * kn_ref[...], cos, sin)
    v = seg(2 * QK_W, DV_W)
    off = 2 * QK_W + DV_W
    gq = seg(off, GK_W) * (GLA_DK ** -0.5)
    gk = seg(off + GK_W, GK_W)
    gv = seg(off + 2 * GK_W, GV_W)
    gg = seg(off + 2 * GK_W + GV_W, GV_W)
    r = jnp.dot(nb, wr_ref[...], preferred_element_type=f32)
    a = jnp.dot(r.astype(bf16), wa2_ref[...], preferred_element_type=f32) + ba_ref[...]
    gl = (jnp.minimum(a, 0.0) - jnp.log1p(jnp.exp(-jnp.abs(a)))) * (1.0 / GLA_GATE_NORM)
    return q, k, v, gq, gk, gv, gl, gg


def _proj_prompt_kernel(x_ref, g1_ref, w_ref, wr_ref, wa2_ref, ba_ref, qn_ref, kn_ref, cos_ref, sin_ref,
                        q_ref, kb_ref, kt_ref, vb_ref, v4_ref, gq_ref, gk_ref, gv_ref, gl_ref, gg_ref):
    tm = x_ref.shape[0]
    q, k, v, gq, gk, gv, gl, gg = _proj_body(x_ref[...], g1_ref, w_ref, wr_ref, wa2_ref, ba_ref, qn_ref, kn_ref,
                                             cos_ref[...], sin_ref[...])
    first_map = lax.broadcasted_iota(jnp.int32, (tm, LANES), 1) < DIFF_DK
    for h in range(DIFF_HEADS):
        qh = q[:, h * LANES:(h + 1) * LANES]
        q_ref[:, (2 * h) * LANES:(2 * h + 1) * LANES] = jnp.where(first_map, qh, 0.0).astype(bf16)
        q_ref[:, (2 * h + 1) * LANES:(2 * h + 2) * LANES] = jnp.where(first_map, 0.0, qh).astype(bf16)
    kb_ref[...] = k.astype(bf16)
    kt_ref[0] = k.T
    vb_ref[...] = v.astype(bf16)
    for h in range(DIFF_HEADS):
        v4_ref[pl.ds(h, tm, stride=DIFF_HEADS), :] = v[:, h * DIFF_DV:(h + 1) * DIFF_DV]
    gq_ref[...] = gq
    gk_ref[...] = gk
    gv_ref[...] = gv.astype(bf16)
    gl_ref[...] = gl
    gg_ref[...] = gg


def _proj_prompt(x2d, g1, w_main, w_r, w_a2, b_a, qn, kn, cos_t, sin_t, *, batch, tm):
    n, d = x2d.shape
    seq = n // batch
    nt = seq // tm
    row = lambda w: pl.BlockSpec((tm, w), lambda b, j: (b * nt + j, 0))
    tab = pl.BlockSpec((tm, LANES), lambda b, j: (j, 0))
    arr = lambda w, dt: jax.ShapeDtypeStruct((n, w), dt)
    out_shape = (
        arr(2 * QK_W, bf16),
        arr(QK_W, bf16),
        jax.ShapeDtypeStruct((batch, QK_W, seq), f32),
        arr(DV_W, bf16),
        jax.ShapeDtypeStruct((n * DIFF_HEADS, DIFF_DV), f32),
        arr(GK_W, f32), arr(GK_W, f32), arr(GV_W, bf16), arr(GK_W, f32), arr(GV_W, f32),
    )
    out_specs = (row(2 * QK_W), row(QK_W), pl.BlockSpec((1, QK_W, tm), lambda b, j: (b, 0, j)), row(DV_W),
                 pl.BlockSpec((tm * DIFF_HEADS, DIFF_DV), lambda b, j: (b * nt + j, 0)),
                 row(GK_W), row(GK_W), row(GV_W), row(GK_W), row(GV_W))
    return pl.pallas_call(
        _proj_prompt_kernel,
        grid=(batch, nt),
        in_specs=[row(d), _const_spec((1, d)), _const_spec(w_main.shape), _const_spec(w_r.shape),
                  _const_spec(w_a2.shape), _const_spec((1, GK_W)), _const_spec((1, QK_W)), _const_spec((1, QK_W)),
                  tab, tab],
        out_specs=out_specs,
        out_shape=out_shape,
        compiler_params=_cparams(("parallel", "parallel")),
        name="proj_prompt",
    )(x2d, g1, w_main, w_r, w_a2, b_a, qn, kn, cos_t, sin_t)


def _proj_sample_kernel(x_ref, g1_ref, w_ref, wr_ref, wa2_ref, ba_ref, qn_ref, kn_ref, cos_ref, sin_ref,
                        q_ref, k_ref, v_ref, gqt_ref, gkt_ref, gv_ref, et_ref, gg_ref):
    q, k, v, gq, gk, gv, gl, gg = _proj_body(x_ref[...], g1_ref, w_ref, wr_ref, wa2_ref, ba_ref, qn_ref, kn_ref,
                                             cos_ref[...], sin_ref[...])
    q_ref[...] = q
    k_ref[...] = k
    v_ref[...] = v
    gv_ref[...] = gv
    gg_ref[...] = gg
    e = jnp.exp(gl)
    tb = gqt_ref.shape[2]
    for blk in range(gqt_ref.shape[0]):
        rows = slice(blk * tb, (blk + 1) * tb)
        gqt_ref[blk] = gq[rows, :].T
        gkt_ref[blk] = gk[rows, :].T
        et_ref[blk] = e[rows, :].T


def _proj_sample(x2d, g1, w_main, w_r, w_a2, b_a, qn, kn, cos_t, sin_t, *, tb):
    n, d = x2d.shape
    flat = lambda w: jax.ShapeDtypeStruct((n, w), f32)
    tr = jax.ShapeDtypeStruct((n // tb, GK_W, tb), f32)
    out_shape = (flat(QK_W), flat(QK_W), flat(DV_W), tr, tr, flat(GV_W), tr, flat(GV_W))
    return pl.pallas_call(
        _proj_sample_kernel,
        in_specs=[pl.BlockSpec(memory_space=pltpu.VMEM)] * 10,
        out_specs=tuple(pl.BlockSpec(memory_space=pltpu.VMEM) for _ in out_shape),
        out_shape=out_shape,
        compiler_params=pltpu.CompilerParams(vmem_limit_bytes=VMEM_LIMIT),
        name="proj_sample",
    )(x2d, g1, w_main, w_r, w_a2, b_a, qn, kn, cos_t, sin_t)


def _attn_kernel(q_ref, kb_ref, v_ref, lqk_ref, o_ref, vb_ref, s_ref, *, tq, lam_init):
    seq = kb_ref.shape[0]
    vb_ref[:, :DIFF_DV] = v_ref[...]
    vb_ref[:, DIFF_DV:] = jnp.ones((seq, LANES), bf16)

    lam = _diff_lambda(lqk_ref[...], lam_init)
    row = lax.broadcasted_iota(jnp.int32, (2 * tq, tq), 0) % tq
    col = lax.broadcasted_iota(jnp.int32, (2 * tq, tq), 1)
    visible = col <= row
    for qi in range(seq // tq):
        rows = slice(qi * tq, (qi + 1) * tq)
        qs = jnp.concatenate([q_ref[rows, :LANES], q_ref[rows, LANES:]], axis=0)
        mx = None
        for kt in range(qi + 1):
            cols = slice(kt * tq, (kt + 1) * tq)
            s = lax.dot_general(qs, kb_ref[cols, :], (((1,), (1,)), ((), ())), preferred_element_type=f32)
            if kt == qi:
                s = jnp.where(visible, s, -jnp.inf)
            s_ref[:, cols] = s
            for c in range(tq // LANES):
                sc = s[:, c * LANES:(c + 1) * LANES]
                mx = sc if mx is None else jnp.maximum(mx, sc)
        m = jnp.broadcast_to(jnp.max(mx, axis=-1, keepdims=True), (2 * tq, tq))
        acc = None
        for kt in range(qi + 1):
            cols = slice(kt * tq, (kt + 1) * tq)
            p = jnp.exp2(s_ref[:, cols] - m).astype(bf16)
            pv = jnp.dot(p, vb_ref[cols, :], preferred_element_type=f32)
            acc = pv if acc is None else acc + pv
        o1 = acc[:tq, :DIFF_DV] / acc[:tq, DIFF_DV:]
        o2 = acc[tq:, :DIFF_DV] / acc[tq:, DIFF_DV:]
        o_ref[rows, :] = o1 - lam * o2


def _attn_prompt(q, kb, vb, lqk, *, batch, seq, tq, lam_init):
    return pl.pallas_call(
        functools.partial(_attn_kernel, tq=tq, lam_init=lam_init),
        grid=(batch, DIFF_HEADS),
        in_specs=[pl.BlockSpec((seq, 2 * LANES), lambda b, h: (b, h)),
                  pl.BlockSpec((seq, LANES), lambda b, h: (b, h)),
                  pl.BlockSpec((seq, LANES), lambda b, h: (b, h)),
                  _const_spec(lqk.shape)],
        out_specs=pl.BlockSpec((seq, DIFF_DV), lambda b, h: (b, h)),
        out_shape=jax.ShapeDtypeStruct((batch * seq, DV_W), f32),
        scratch_shapes=[pltpu.VMEM((seq, 2 * LANES), bf16), pltpu.VMEM((2 * tq, seq), f32)],
        compiler_params=_cparams(("parallel", "parallel")),
        name="attn_prompt",
    )(q, kb, vb, lqk)


def _cumsum_rows(g, tril):
    g1 = g.astype(bf16)
    r1 = g - g1.astype(f32)
    g2 = r1.astype(bf16)
    g3 = (r1 - g2.astype(f32)).astype(bf16)
    dot = lambda t: jnp.dot(tril, t, preferred_element_type=f32)
    return dot(g1) + dot(g2) + dot(g3)


def _gla_kernel(q_ref, k_ref, v_ref, g_ref, o_ref, s_ref, st_ref, *, tg):
    step = pl.program_id(1)

    @pl.when(step == 0)
    def _():
        st_ref[...] = jnp.zeros(st_ref.shape, f32)

    c = GLA_CHUNK
    ri = lax.broadcasted_iota(jnp.int32, (c, c), 0)
    ci = lax.broadcasted_iota(jnp.int32, (c, c), 1)
    causal = ri >= ci
    tril = jnp.where(causal, 1.0, 0.0).astype(bf16)
    lane = lax.broadcasted_iota(jnp.int32, (c, LANES), 1)
    head0 = lane < GLA_DK
    lane_sq = lax.broadcasted_iota(jnp.int32, (LANES, LANES), 1) < GLA_DK

    def chunk(ic, carry):
        r0 = pl.multiple_of(ic * c, c)
        rows = pl.ds(r0, c)
        b = _cumsum_rows(g_ref[rows, :], tril)
        q = q_ref[rows, :]
        k = k_ref[rows, :]
        b_mid = b[c // 2 - 1:c // 2, :]
        b_last = b[c - 1:c, :]
        qe = q * jnp.exp(b - b_mid)
        ke = (k * jnp.exp(b_mid - b)).astype(bf16)
        qb = q * jnp.exp(b)
        kd = (k * jnp.exp(b_last - b)).astype(bf16)
        e_last = jnp.exp(b_last)
        for p in range(GLA_HEADS // 2):
            sl = slice(p * LANES, (p + 1) * LANES)
            st = st_ref[p]
            stb = st.astype(bf16)
            upd = []
            for j in range(2):
                h = 2 * p + j
                mine = head0 if j == 0 else jnp.logical_not(head0)
                qe_h = jnp.where(mine, qe[:, sl], 0.0).astype(bf16)
                qb_h = jnp.where(mine, qb[:, sl], 0.0).astype(bf16)
                v_h = v_ref[rows, h * GLA_DV:(h + 1) * GLA_DV]
                a = lax.dot_general(qe_h, ke[:, sl], (((1,), (1,)), ((), ())), preferred_element_type=f32)
                a = jnp.where(causal, a, 0.0).astype(bf16)
                o = jnp.dot(a, v_h, preferred_element_type=f32)
                o = o + lax.dot_general(qb_h, stb, (((1,), (1,)), ((), ())), preferred_element_type=f32)
                o_ref[rows, h * GLA_DV:(h + 1) * GLA_DV] = o
                upd.append(lax.dot_general(v_h, kd[:, sl], (((0,), (0,)), ((), ())), preferred_element_type=f32))
            st_ref[p] = st * e_last[:, sl] + jnp.where(lane_sq, upd[0], upd[1])
        return carry

    lax.fori_loop(0, tg // c, chunk, 0, unroll=True)

    @pl.when(step == pl.num_programs(1) - 1)
    def _():
        for p in range(GLA_HEADS // 2):
            t = st_ref[p].T
            s_ref[0, 2 * p] = t[:GLA_DK, :]
            s_ref[0, 2 * p + 1] = t[GLA_DK:, :]


def _gla_prompt(gq, gk, gv, gl, *, batch, seq, tg):
    ns = seq // tg
    row = lambda w: pl.BlockSpec((tg, w), lambda b, i: (b * ns + i, 0))
    return pl.pallas_call(
        functools.partial(_gla_kernel, tg=tg),
        grid=(batch, ns),
        in_specs=[row(GK_W), row(GK_W), row(GV_W), row(GK_W)],
        out_specs=(row(GV_W), pl.BlockSpec((1, GLA_HEADS, GLA_DK, GLA_DV), lambda b, i: (b, 0, 0, 0))),
        out_shape=(jax.ShapeDtypeStruct((batch * seq, GV_W), f32),
                   jax.ShapeDtypeStruct((batch, GLA_HEADS, GLA_DK, GLA_DV), f32)),
        scratch_shapes=[pltpu.VMEM((GLA_HEADS // 2, LANES, LANES), f32)],
        compiler_params=_cparams(("parallel", "arbitrary")),
        name="gla_prompt",
    )(gq, gk, gv, gl)


def _post_kernel(x_ref, od_ref, og_ref, gg_ref, sub_ref, gn_ref, wo_ref, n2_ref, wgu_ref, wd_ref, y_ref,
                 *, lam_init, d_ff, ff_chunk):
    od = od_ref[...]
    odn = od * _group_scale(od, DIFF_DV) * sub_ref[...] * (1.0 - lam_init)
    og = og_ref[...]
    ogn = og * _group_scale(og, GLA_DV) * gn_ref[...]
    gg = gg_ref[...]
    ogn = ogn * (gg * jax.nn.sigmoid(gg))
    mix = jnp.concatenate([odn, ogn], axis=-1).astype(bf16)
    y = x_ref[...] + jnp.dot(mix, wo_ref[...], preferred_element_type=f32)

    n2 = (y * lax.rsqrt(jnp.mean(y * y, axis=-1, keepdims=True) + EPS) * n2_ref[...]).astype(bf16)
    acc = y
    for c0 in range(0, d_ff, ff_chunk):
        a = jnp.dot(n2, wgu_ref[:, c0:c0 + ff_chunk], preferred_element_type=f32)
        b = jnp.dot(n2, wgu_ref[:, d_ff + c0:d_ff + c0 + ff_chunk], preferred_element_type=f32)
        hid = (a * jax.nn.sigmoid(a) * b).astype(bf16)
        acc = acc + jnp.dot(hid, wd_ref[c0:c0 + ff_chunk, :], preferred_element_type=f32)
    y_ref[...] = acc


def _post(x2d, od, og, gg, sub, gn, w_out, n2, w_gu, w_down, *, tm, lam_init):
    n, d = x2d.shape
    d_ff = w_down.shape[0]
    row = lambda w: pl.BlockSpec((tm, w), lambda i: (i, 0))
    return pl.pallas_call(
        functools.partial(_post_kernel, lam_init=lam_init, d_ff=d_ff, ff_chunk=2 * LANES),
        grid=(n // tm,),
        in_specs=[row(d), row(DV_W), row(GV_W), row(GV_W), _const_spec((1, DV_W)), _const_spec((1, GV_W)),
                  _const_spec(w_out.shape), _const_spec((1, d)), _const_spec(w_gu.shape), _const_spec(w_down.shape)],
        out_specs=row(d),
        out_shape=jax.ShapeDtypeStruct((n, d), f32),
        compiler_params=_cparams(("parallel",)),
        name="post",
    )(x2d, od, og, gg, sub, gn, w_out, n2, w_gu, w_down)


def _attn_sample_kernel(pt_ref, q_ref, kn_ref, vn_ref, lqk_ref, ck_hbm, cv_hbm, o_ref, kbuf, vbuf, sem,
                        *, layer, n_pages, page, lam_init):
    b = pl.program_id(0)
    nb = pl.num_programs(0)

    def copies(sample, slot):
        out = []
        for j in range(n_pages):
            phys = pt_ref[sample * n_pages + j]
            out.append(pltpu.make_async_copy(ck_hbm.at[layer, phys], kbuf.at[slot, j], sem.at[0, slot]))
            out.append(pltpu.make_async_copy(cv_hbm.at[layer, phys], vbuf.at[slot, j], sem.at[1, slot]))
        return out

    @pl.when(b == 0)
    def _():
        for cp in copies(0, 0):
            cp.start()

    slot = b % 2

    @pl.when(b + 1 < nb)
    def _():
        for cp in copies(b + 1, 1 - slot):
            cp.start()

    for cp in copies(b, slot):
        cp.wait()

    past = n_pages * page
    nrow = 2 * DIFF_HEADS
    q = q_ref[0]
    qb = q.astype(bf16)
    rid = lax.broadcasted_iota(jnp.int32, (nrow, past), 0)
    s = jnp.zeros((nrow, past), f32)
    for h in range(DIFF_HEADS):
        for mp in range(2):
            kt = jnp.concatenate([kbuf[slot, j, h, mp] for j in range(n_pages)], axis=-1)
            sr = jnp.dot(qb, kt.astype(bf16), preferred_element_type=f32)
            s = jnp.where(rid == 2 * h + mp, sr, s)
    s_new = jnp.sum(q * kn_ref[0], axis=-1, keepdims=True)
    m = jnp.maximum(jnp.max(s, axis=-1, keepdims=True), s_new)
    p = jnp.exp2(s - m)
    p_new = jnp.exp2(s_new - m)
    inv_l = 1.0 / (jnp.sum(p, axis=-1, keepdims=True) + p_new)
    pb = p.astype(bf16)
    lam = _diff_lambda(lqk_ref[...], lam_init)
    vn = vn_ref[0]
    heads = []
    for h in range(DIFF_HEADS):
        vh = jnp.concatenate([vbuf[slot, j, pl.ds(h, page, stride=DIFF_HEADS), :] for j in range(n_pages)], axis=0)
        oh = jnp.dot(pb, vh.astype(bf16), preferred_element_type=f32)
        oh = (oh + p_new * vn[:, h * DIFF_DV:(h + 1) * DIFF_DV]) * inv_l
        heads.append(oh[2 * h:2 * h + 1, :] - lam * oh[2 * h + 1:2 * h + 2, :])
    o_ref[0] = jnp.concatenate(heads, axis=-1)


def _attn_sample(page_table, q, k_new, v_new, lqk, cache_kt, cache_vr, *, layer, lam_init):
    nb, n_pages = page_table.shape
    page = cache_kt.shape[-1]
    nrow = 2 * DIFF_HEADS
    grid_spec = pltpu.PrefetchScalarGridSpec(
        num_scalar_prefetch=1,
        grid=(nb,),
        in_specs=[pl.BlockSpec((1, nrow, DIFF_DK), lambda b, pt: (b, 0, 0)),
                  pl.BlockSpec((1, nrow, DIFF_DK), lambda b, pt: (b, 0, 0)),
                  pl.BlockSpec((1, 1, DV_W), lambda b, pt: (b, 0, 0)),
                  pl.BlockSpec(lqk.shape, lambda b, pt: (0, 0)),
                  pl.BlockSpec(memory_space=pl.ANY), pl.BlockSpec(memory_space=pl.ANY)],
        out_specs=pl.BlockSpec((1, 1, DV_W), lambda b, pt: (b, 0, 0)),
        scratch_shapes=[pltpu.VMEM((2, n_pages) + cache_kt.shape[2:], f32),
                        pltpu.VMEM((2, n_pages) + cache_vr.shape[2:], f32),
                        pltpu.SemaphoreType.DMA((2, 2))],
    )
    out = pl.pallas_call(
        functools.partial(_attn_sample_kernel, layer=layer, n_pages=n_pages, page=page, lam_init=lam_init),
        grid_spec=grid_spec,
        out_shape=jax.ShapeDtypeStruct((nb, 1, DV_W), f32),
        compiler_params=_cparams(("arbitrary",)),
        name="attn_sample",
    )(page_table.reshape(-1), q.reshape(nb, nrow, DIFF_DK), k_new.reshape(nb, nrow, DIFF_DK),
      v_new.reshape(nb, 1, DV_W), lqk, cache_kt, cache_vr)
    return out.reshape(nb, DV_W)


def _gla_sample_kernel(qt_ref, kt_ref, v_ref, et_ref, s0_ref, o_ref, s_ref, *, tb):
    qt = qt_ref[0]
    kt = kt_ref[0]
    et = et_ref[0]
    v = v_ref[...]
    for i in range(tb):
        for h in range(GLA_HEADS):
            ks = slice(h * GLA_DK, (h + 1) * GLA_DK)
            vs = slice(h * GLA_DV, (h + 1) * GLA_DV)
            s_new = s0_ref[i, h] * et[ks, i:i + 1] + kt[ks, i:i + 1] * v[i:i + 1, vs]
            s_ref[i, h] = s_new
            o_ref[i:i + 1, vs] = jnp.sum(qt[ks, i:i + 1] * s_new, axis=0, keepdims=True)


def _gla_sample(gqt, gkt, gv, et, s0):
    nblk, _, tb = gqt.shape
    tr = pl.BlockSpec((1, GK_W, tb), lambda i: (i, 0, 0))
    row = pl.BlockSpec((tb, GV_W), lambda i: (i, 0))
    st = pl.BlockSpec((tb, GLA_HEADS, GLA_DK, GLA_DV), lambda i: (i, 0, 0, 0))
    return pl.pallas_call(
        functools.partial(_gla_sample_kernel, tb=tb),
        grid=(nblk,),
        in_specs=[tr, tr, row, tr, st],
        out_specs=(row, st),
        out_shape=(jax.ShapeDtypeStruct((nblk * tb, GV_W), f32), jax.ShapeDtypeStruct(s0.shape, f32)),
        compiler_params=_cparams(("parallel",)),
        name="gla_sample",
    )(gqt, gkt, gv, et, s0)


def _rope_tables(pos):
    half = DIFF_DK // 2
    freqs = ROPE_THETA ** (-jnp.arange(half, dtype=f32) / half)
    ang = pos.astype(f32)[:, None] * freqs[None, :]
    cos, sin = jnp.cos(ang), jnp.sin(ang)
    reps = LANES // DIFF_DK
    return jnp.tile(jnp.concatenate([cos, cos], -1), (1, reps)), jnp.tile(jnp.concatenate([-sin, sin], -1), (1, reps))


def kernel(x_prompt, x_sample, cache_k, cache_v, state_gla, page_table, norm1, w_in, q_norm, k_norm, lambda_qk,
           subln, w_a2, b_a, gla_norm, w_out, norm2, w_gu, w_down):
    batch, seq, d = x_prompt.shape
    nb, dec_seq, _ = x_sample.shape
    assert dec_seq == 1
    depth = w_in.shape[0]
    n_pages = page_table.shape[1]
    page = cache_k.shape[2]
    past = n_pages * page
    rank = w_a2.shape[1]

    cos_p, sin_p = _rope_tables(jnp.arange(seq))
    cos_s, sin_s = _rope_tables(past + jnp.arange(dec_seq))
    cos_s = jnp.broadcast_to(cos_s, (nb, LANES))
    sin_s = jnp.broadcast_to(sin_s, (nb, LANES))

    ck = jnp.transpose(cache_k, (0, 1, 3, 4, 5, 2))
    cv = cache_v.reshape(depth, -1, page * DIFF_HEADS, DIFF_DV)

    yp = x_prompt.reshape(batch * seq, d)
    ys = x_sample.reshape(nb, d)
    kp, vp, sp, k_s, v_s, s_s = [], [], [], [], [], []
    for l in range(depth):
        lam_init = 0.8 - 0.6 * math.exp(-0.3 * l)
        w_main = w_in[l, :, :MAIN_W].astype(bf16)
        w_r = jnp.pad(w_in[l, :, MAIN_W:], ((0, 0), (0, LANES - rank))).astype(bf16)
        wa2 = jnp.pad(w_a2[l], ((0, LANES - rank), (0, 0))).astype(bf16)
        proj_args = (norm1[l][None], w_main, w_r, wa2, b_a[l][None],
                     jnp.tile(q_norm[l], QK_W // DIFF_DK)[None], jnp.tile(k_norm[l], QK_W // DIFF_DK)[None])
        post_args = (jnp.tile(subln[l], DIFF_HEADS)[None], jnp.tile(gla_norm[l], GLA_HEADS)[None],
                     w_out[l].astype(bf16), norm2[l][None], w_gu[l].astype(bf16), w_down[l].astype(bf16))

        q, kb, kt, vb, v4, gq, gk, gv, gl, gg = _proj_prompt(yp, *proj_args, cos_p, sin_p, batch=batch, tm=512)
        od = _attn_prompt(q, kb, vb, lambda_qk[l], batch=batch, seq=seq, tq=512, lam_init=lam_init)
        og, s_fin = _gla_prompt(gq, gk, gv, gl, batch=batch, seq=seq, tg=512)
        yp = _post(yp, od, og, gg, *post_args, tm=512, lam_init=lam_init)
        kp.append(kt)
        vp.append(v4)
        sp.append(s_fin)

        q, k, v, gqt, gkt, gv, et, gg = _proj_sample(ys, *proj_args, cos_s, sin_s, tb=GLA_SAMPLE_BLOCK)
        od = _attn_sample(page_table, q, k, v, lambda_qk[l], ck, cv, layer=l, lam_init=lam_init)
        og, s_new = _gla_sample(gqt, gkt, gv, et, state_gla[l])
        ys = _post(ys, od, og, gg, *post_args, tm=nb, lam_init=lam_init)
        k_s.append(k.reshape(nb, dec_seq, DIFF_HEADS, 2, DIFF_DK))
        v_s.append(v.reshape(nb, dec_seq, DIFF_HEADS, DIFF_DV))
        s_s.append(s_new)

    k_prompt = jnp.stack(kp).reshape(depth, batch, DIFF_HEADS, 2, DIFF_DK, seq).transpose(0, 1, 5, 2, 3, 4)
    v_prompt = jnp.stack(vp).reshape(depth, batch, seq, DIFF_HEADS, DIFF_DV)
    return (yp.reshape(batch, seq, d), ys.reshape(nb, dec_seq, d), k_prompt, v_prompt, jnp.stack(sp),
            jnp.stack(k_s), jnp.stack(v_s), jnp.stack(s_s))
```

```python
import functools
import math

import jax
import jax.numpy as jnp
from jax import lax
from jax.experimental import pallas as pl
from jax.experimental.pallas import tpu as pltpu

f32 = jnp.float32
bf16 = jnp.bfloat16

DIFF_HEADS = 4
DIFF_DK = 64
DIFF_DV = 128
GLA_HEADS = 4
GLA_DK = 64
GLA_DV = 128
GLA_GATE_NORM = 16.0
ROPE_THETA = 10000.0
EPS = 1e-6
LOG2_E = math.log2(math.e)

QK_W = DIFF_HEADS * 2 * DIFF_DK
DV_W = DIFF_HEADS * DIFF_DV
GK_W = GLA_HEADS * GLA_DK
GV_W = GLA_HEADS * GLA_DV
MAIN_W = 2 * QK_W + DV_W + 2 * GK_W + 2 * GV_W

LANES = 128
GLA_SAMPLE_BLOCK = 32
GLA_BLOCK = 256
GLA_SAFE_LOG_DECAY = 80.0
VMEM_LIMIT = 56 * 1024 * 1024
FUSED_VMEM_LIMIT = 60 * 1024 * 1024


def _cparams(sem):
    return pltpu.CompilerParams(dimension_semantics=sem, vmem_limit_bytes=VMEM_LIMIT)


def _const_spec(shape):
    nd = len(shape)
    return pl.BlockSpec(shape, lambda *_: (0,) * nd, pipeline_mode=pl.Buffered(1))


def _group_scale(z, width):
    lane = lax.broadcasted_iota(jnp.int32, (z.shape[0], LANES), 1)
    cols = []
    for c in range(z.shape[1] // LANES):
        zc = z[:, c * LANES:(c + 1) * LANES]
        zz = zc * zc
        if width == LANES:
            cols.append(jnp.broadcast_to(lax.rsqrt(jnp.mean(zz, axis=-1, keepdims=True) + EPS), zc.shape))
        else:
            lo = lane < width
            s_lo = jnp.sum(jnp.where(lo, zz, 0.0), axis=-1, keepdims=True)
            s_hi = jnp.sum(jnp.where(lo, 0.0, zz), axis=-1, keepdims=True)
            r_lo = lax.rsqrt(s_lo * (1.0 / width) + EPS)
            r_hi = lax.rsqrt(s_hi * (1.0 / width) + EPS)
            cols.append(jnp.where(lo, r_lo, r_hi))
    return jnp.concatenate(cols, axis=-1)


def _rope(z, cos, sin_signed):
    lane = lax.broadcasted_iota(jnp.int32, (z.shape[0], LANES), 1)
    first_half = (lane % DIFF_DK) < (DIFF_DK // 2)
    cols = []
    for c in range(z.shape[1] // LANES):
        zc = z[:, c * LANES:(c + 1) * LANES]
        partner = jnp.where(first_half, pltpu.roll(zc, LANES - DIFF_DK // 2, 1), pltpu.roll(zc, DIFF_DK // 2, 1))
        cols.append(zc * cos + partner * sin_signed)
    return jnp.concatenate(cols, axis=-1)


def _diff_lambda(lqk, lam_init):
    a = jnp.sum(lqk[0:1, :] * lqk[1:2, :], axis=-1, keepdims=True)
    b = jnp.sum(lqk[2:3, :] * lqk[3:4, :], axis=-1, keepdims=True)
    return jnp.exp(a) - jnp.exp(b) + lam_init


def _proj_body(x, g1_ref, w_ref, wr_ref, wa2_ref, ba_ref, qn_ref, kn_ref, cos, sin):
    n = x * lax.rsqrt(jnp.mean(x * x, axis=-1, keepdims=True) + EPS) * g1_ref[...]
    nb = n.astype(bf16)

    def seg(lo, width):
        return jnp.dot(nb, w_ref[:, lo:lo + width], preferred_element_type=f32)

    zq = seg(0, QK_W)
    q = _rope(zq * _group_scale(zq, DIFF_DK) * qn_ref[...], cos, sin) * (DIFF_DK ** -0.5 * LOG2_E)
    zk = seg(QK_W, QK_W)
    k = _rope(zk * _group_scale(zk, DIFF_DK) * kn_ref[...], cos, sin)
    v = seg(2 * QK_W, DV_W)
    off = 2 * QK_W + DV_W
    gq = seg(off, GK_W) * (GLA_DK ** -0.5)
    gk = seg(off + GK_W, GK_W)
    gv = seg(off + 2 * GK_W, GV_W)
    gg = seg(off + 2 * GK_W + GV_W, GV_W)
    r = jnp.dot(nb, wr_ref[...], preferred_element_type=f32)
    a = jnp.dot(r.astype(bf16), wa2_ref[...], preferred_element_type=f32) + ba_ref[...]
    gl = (jnp.minimum(a, 0.0) - jnp.log1p(jnp.exp(-jnp.abs(a)))) * (1.0 / GLA_GATE_NORM)
    return q, k, v, gq, gk, gv, gl, gg


def _proj_prompt_kernel(x_ref, g1_ref, w_ref, wr_ref, wa2_ref, ba_ref, qn_ref, kn_ref, cos_ref, sin_ref,
                        q_ref, kb_ref, kt_ref, vb_ref, v4_ref, gq_ref, gk_ref, gv_ref, gl_ref, gg_ref):
    tm = x_ref.shape[0]
    q, k, v, gq, gk, gv, gl, gg = _proj_body(x_ref[...], g1_ref, w_ref, wr_ref, wa2_ref, ba_ref, qn_ref, kn_ref,
                                             cos_ref[...], sin_ref[...])
    first_map = lax.broadcasted_iota(jnp.int32, (tm, LANES), 1) < DIFF_DK
    for h in range(DIFF_HEADS):
        qh = q[:, h * LANES:(h + 1) * LANES]
        q_ref[:, (2 * h) * LANES:(2 * h + 1) * LANES] = jnp.where(first_map, qh, 0.0).astype(bf16)
        q_ref[:, (2 * h + 1) * LANES:(2 * h + 2) * LANES] = jnp.where(first_map, 0.0, qh).astype(bf16)
    kb_ref[...] = k.astype(bf16)
    kt_ref[0] = k.T
    vb_ref[...] = v.astype(bf16)
    for h in range(DIFF_HEADS):
        v4_ref[pl.ds(h, tm, stride=DIFF_HEADS), :] = v[:, h * DIFF_DV:(h + 1) * DIFF_DV]
    gq_ref[...] = gq
    gk_ref[...] = gk
    gv_ref[...] = gv.astype(bf16)
    gl_ref[...] = gl
    gg_ref[...] = gg


def _proj_prompt(x2d, g1, w_main, w_r, w_a2, b_a, qn, kn, cos_t, sin_t, *, batch, tm):
    n, d = x2d.shape
    seq = n // batch
    nt = seq // tm
    row = lambda w: pl.BlockSpec((tm, w), lambda b, j: (b * nt + j, 0))
    tab = pl.BlockSpec((tm, LANES), lambda b, j: (j, 0))
    arr = lambda w, dt: jax.ShapeDtypeStruct((n, w), dt)
    out_shape = (
        arr(2 * QK_W, bf16),
        arr(QK_W, bf16),
        jax.ShapeDtypeStruct((batch, QK_W, seq), f32),
        arr(DV_W, bf16),
        jax.ShapeDtypeStruct((n * DIFF_HEADS, DIFF_DV), f32),
        arr(GK_W, f32), arr(GK_W, f32), arr(GV_W, bf16), arr(GK_W, f32), arr(GV_W, f32),
    )
    out_specs = (row(2 * QK_W), row(QK_W), pl.BlockSpec((1, QK_W, tm), lambda b, j: (b, 0, j)), row(DV_W),
                 pl.BlockSpec((tm * DIFF_HEADS, DIFF_DV), lambda b, j: (b * nt + j, 0)),
                 row(GK_W), row(GK_W), row(GV_W), row(GK_W), row(GV_W))
    return pl.pallas_call(
        _proj_prompt_kernel,
        grid=(batch, nt),
        in_specs=[row(d), _const_spec((1, d)), _const_spec(w_main.shape), _const_spec(w_r.shape),
                  _const_spec(w_a2.shape), _const_spec((1, GK_W)), _const_spec((1, QK_W)), _const_spec((1, QK_W)),
                  tab, tab],
        out_specs=out_specs,
        out_shape=out_shape,
        compiler_params=_cparams(("parallel", "parallel")),
        name="proj_prompt",
    )(x2d, g1, w_main, w_r, w_a2, b_a, qn, kn, cos_t, sin_t)


def _proj_sample_kernel(x_ref, g1_ref, w_ref, wr_ref, wa2_ref, ba_ref, qn_ref, kn_ref, cos_ref, sin_ref,
                        q_ref, k_ref, v_ref, gqt_ref, gkt_ref, gv_ref, et_ref, gg_ref):
    q, k, v, gq, gk, gv, gl, gg = _proj_body(x_ref[...], g1_ref, w_ref, wr_ref, wa2_ref, ba_ref, qn_ref, kn_ref,
                                             cos_ref[...], sin_ref[...])
    q_ref[...] = q
    k_ref[...] = k
    v_ref[...] = v
    gv_ref[...] = gv
    gg_ref[...] = gg
    e = jnp.exp(gl)
    tb = gqt_ref.shape[2]
    for blk in range(gqt_ref.shape[0]):
        rows = slice(blk * tb, (blk + 1) * tb)
        gqt_ref[blk] = gq[rows, :].T
        gkt_ref[blk] = gk[rows, :].T
        et_ref[blk] = e[rows, :].T


def _proj_sample(x2d, g1, w_main, w_r, w_a2, b_a, qn, kn, cos_t, sin_t, *, tb):
    n, d = x2d.shape
    flat = lambda w: jax.ShapeDtypeStruct((n, w), f32)
    tr = jax.ShapeDtypeStruct((n // tb, GK_W, tb), f32)
    out_shape = (flat(QK_W), flat(QK_W), flat(DV_W), tr, tr, flat(GV_W), tr, flat(GV_W))
    return pl.pallas_call(
        _proj_sample_kernel,
        in_specs=[pl.BlockSpec(memory_space=pltpu.VMEM)] * 10,
        out_specs=tuple(pl.BlockSpec(memory_space=pltpu.VMEM) for _ in out_shape),
        out_shape=out_shape,
        compiler_params=pltpu.CompilerParams(vmem_limit_bytes=VMEM_LIMIT),
        name="proj_sample",
    )(x2d, g1, w_main, w_r, w_a2, b_a, qn, kn, cos_t, sin_t)


def _attn_kernel(q_ref, kb_ref, v_ref, lqk_ref, o_ref, vb_ref, s_ref, *, tq, lam_init):
    seq = kb_ref.shape[0]
    vb_ref[:, :DIFF_DV] = v_ref[...]
    vb_ref[:, DIFF_DV:] = jnp.ones((seq, LANES), bf16)

    lam = _diff_lambda(lqk_ref[...], lam_init)
    row = lax.broadcasted_iota(jnp.int32, (2 * tq, tq), 0) % tq
    col = lax.broadcasted_iota(jnp.int32, (2 * tq, tq), 1)
    visible = col <= row
    for qi in range(seq // tq):
        rows = slice(qi * tq, (qi + 1) * tq)
        qs = jnp.concatenate([q_ref[rows, :LANES], q_ref[rows, LANES:]], axis=0)
        mx = None
        for kt in range(qi + 1):
            cols = slice(kt * tq, (kt + 1) * tq)
            s = lax.dot_general(qs, kb_ref[cols, :], (((1,), (1,)), ((), ())), preferred_element_type=f32)
            if kt == qi:
                s = jnp.where(visible, s, -jnp.inf)
            s_ref[:, cols] = s
            for c in range(tq // LANES):
                sc = s[:, c * LANES:(c + 1) * LANES]
                mx = sc if mx is None else jnp.maximum(mx, sc)
        m = jnp.broadcast_to(jnp.max(mx, axis=-1, keepdims=True), (2 * tq, tq))
        acc = None
        for kt in range(qi + 1):
            cols = slice(kt * tq, (kt + 1) * tq)
            p = jnp.exp2(s_ref[:, cols] - m).astype(bf16)
            pv = jnp.dot(p, vb_ref[cols, :], preferred_element_type=f32)
            acc = pv if acc is None else acc + pv
        o1 = acc[:tq, :DIFF_DV] / acc[:tq, DIFF_DV:]
        o2 = acc[tq:, :DIFF_DV] / acc[tq:, DIFF_DV:]
        o_ref[rows, :] = o1 - lam * o2


def _attn_prompt(q, kb, vb, lqk, *, batch, seq, tq, lam_init):
    return pl.pallas_call(
        functools.partial(_attn_kernel, tq=tq, lam_init=lam_init),
        grid=(batch, DIFF_HEADS),
        in_specs=[pl.BlockSpec((seq, 2 * LANES), lambda b, h: (b, h)),
                  pl.BlockSpec((seq, LANES), lambda b, h: (b, h)),
                  pl.BlockSpec((seq, LANES), lambda b, h: (b, h)),
                  _const_spec(lqk.shape)],
        out_specs=pl.BlockSpec((seq, DIFF_DV), lambda b, h: (b, h)),
        out_shape=jax.ShapeDtypeStruct((batch * seq, DV_W), f32),
        scratch_shapes=[pltpu.VMEM((seq, 2 * LANES), bf16), pltpu.VMEM((2 * tq, seq), f32)],
        compiler_params=_cparams(("parallel", "parallel")),
        name="attn_prompt",
    )(q, kb, vb, lqk)


def _cumsum_rows(g, tril):
    g1 = g.astype(bf16)
    r1 = g - g1.astype(f32)
    g2 = r1.astype(bf16)
    g3 = (r1 - g2.astype(f32)).astype(bf16)
    dot = lambda t: jnp.dot(tril, t, preferred_element_type=f32)
    return dot(g1) + dot(g2) + dot(g3)


def _gla_kernel(q_ref, k_ref, v_ref, g_ref, o_ref, s_ref, st_ref, b_ref, *, tg):
    step = pl.program_id(1)

    @pl.when(step == 0)
    def _():
        st_ref[...] = jnp.zeros(st_ref.shape, f32)

    c = GLA_BLOCK
    nblock = tg // c
    ri = lax.broadcasted_iota(jnp.int32, (c, c), 0)
    ci = lax.broadcasted_iota(jnp.int32, (c, c), 1)
    causal = ri >= ci
    tril = jnp.where(causal, 1.0, 0.0).astype(bf16)
    head0 = lax.broadcasted_iota(jnp.int32, (c, LANES), 1) < GLA_DK
    lane_sq = lax.broadcasted_iota(jnp.int32, (LANES, LANES), 1) < GLA_DK
    block_rows = lambda ib: slice(ib * c, (ib + 1) * c)
    head_cols = lambda h: slice(h * GLA_DV, (h + 1) * GLA_DV)
    head_mask = lambda j: head0 if j == 0 else jnp.logical_not(head0)

    worst = None
    for ib in range(nblock):
        b = _cumsum_rows(g_ref[block_rows(ib), :], tril)
        b_ref[block_rows(ib), :] = b
        b_mid = b[c // 2 - 1:c // 2, :]
        w = jnp.maximum(-b_mid, b_mid - b[c - 1:c, :])
        worst = w if worst is None else jnp.maximum(worst, w)
    safe = jnp.max(worst) < GLA_SAFE_LOG_DECAY

    @pl.when(safe)
    def _():
        for ib in range(nblock):
            rows = block_rows(ib)
            b = b_ref[rows, :]
            b_mid = b[c // 2 - 1:c // 2, :]
            qe = q_ref[rows, :] * jnp.exp(b - b_mid)
            ke = (k_ref[rows, :] * jnp.exp(b_mid - b)).astype(bf16)
            for h in range(GLA_HEADS):
                sl = slice((h // 2) * LANES, (h // 2 + 1) * LANES)
                qe_h = jnp.where(head_mask(h % 2), qe[:, sl], 0.0).astype(bf16)
                a = lax.dot_general(qe_h, ke[:, sl], (((1,), (1,)), ((), ())), preferred_element_type=f32)
                a = jnp.where(causal, a, 0.0).astype(bf16)
                o_ref[rows, head_cols(h)] = jnp.dot(a, v_ref[rows, head_cols(h)], preferred_element_type=f32)

    @pl.when(jnp.logical_not(safe))
    def _():
        tok = lax.broadcasted_iota(jnp.int32, (c, 1), 0)

        def exact_block(ib, carry):
            rows = pl.ds(pl.multiple_of(ib * c, c), c)
            b = b_ref[rows, :]
            q = q_ref[rows, :]
            k = k_ref[rows, :]
            v = v_ref[rows, :].astype(f32)
            pick = lambda x, s: jnp.sum(jnp.where(tok == s, x, 0.0), axis=0, keepdims=True)

            def key(s, acc):
                w = q * pick(k, s) * jnp.exp(jnp.minimum(b - pick(b, s), 0.0))
                w = jnp.where(tok >= s, w, 0.0)
                vs = pick(v, s)
                return tuple(
                    acc[h] + jnp.sum(w[:, h * GLA_DK:(h + 1) * GLA_DK], axis=-1, keepdims=True) * vs[:, head_cols(h)]
                    for h in range(GLA_HEADS))

            acc = lax.fori_loop(0, c, key, tuple(jnp.zeros((c, GLA_DV), f32) for _ in range(GLA_HEADS)))
            for h in range(GLA_HEADS):
                o_ref[rows, head_cols(h)] = acc[h]
            return carry

        lax.fori_loop(0, nblock, exact_block, 0)

    for ib in range(nblock):
        rows = block_rows(ib)
        b = b_ref[rows, :]
        b_last = b[c - 1:c, :]
        qb = q_ref[rows, :] * jnp.exp(b)
        kd = (k_ref[rows, :] * jnp.exp(b_last - b)).astype(bf16)
        e_last = jnp.exp(b_last)
        for p in range(GLA_HEADS // 2):
            sl = slice(p * LANES, (p + 1) * LANES)
            st = st_ref[p]
            stb = st.astype(bf16)
            upd = []
            for j in range(2):
                h = 2 * p + j
                qb_h = jnp.where(head_mask(j), qb[:, sl], 0.0).astype(bf16)
                o_ref[rows, head_cols(h)] += lax.dot_general(qb_h, stb, (((1,), (1,)), ((), ())),
                                                             preferred_element_type=f32)
                upd.append(lax.dot_general(v_ref[rows, head_cols(h)], kd[:, sl], (((0,), (0,)), ((), ())),
                                           preferred_element_type=f32))
            st_ref[p] = st * e_last[:, sl] + jnp.where(lane_sq, upd[0], upd[1])

    @pl.when(step == pl.num_programs(1) - 1)
    def _():
        for p in range(GLA_HEADS // 2):
            t = st_ref[p].T
            s_ref[0, 2 * p] = t[:GLA_DK, :]
            s_ref[0, 2 * p + 1] = t[GLA_DK:, :]


def _gla_prompt(gq, gk, gv, gl, *, batch, seq, tg):
    ns = seq // tg
    row = lambda w: pl.BlockSpec((tg, w), lambda b, i: (b * ns + i, 0))
    return pl.pallas_call(
        functools.partial(_gla_kernel, tg=tg),
        grid=(batch, ns),
        in_specs=[row(GK_W), row(GK_W), row(GV_W), row(GK_W)],
        out_specs=(row(GV_W), pl.BlockSpec((1, GLA_HEADS, GLA_DK, GLA_DV), lambda b, i: (b, 0, 0, 0))),
        out_shape=(jax.ShapeDtypeStruct((batch * seq, GV_W), f32),
                   jax.ShapeDtypeStruct((batch, GLA_HEADS, GLA_DK, GLA_DV), f32)),
        scratch_shapes=[pltpu.VMEM((GLA_HEADS // 2, LANES, LANES), f32), pltpu.VMEM((tg, GK_W), f32)],
        compiler_params=_cparams(("parallel", "arbitrary")),
        name="gla_prompt",
    )(gq, gk, gv, gl)


def _merge_ffn_pieces(x_ref, od_ref, og_ref, gg_ref, sub_ref, gn_ref, wo_ref, n2_ref, lam_init):
    od = od_ref[...]
    odn = od * _group_scale(od, DIFF_DV) * sub_ref[...] * (1.0 - lam_init)
    og = og_ref[...]
    ogn = og * _group_scale(og, GLA_DV) * gn_ref[...]
    gg = gg_ref[...]
    ogn = ogn * (gg * jax.nn.sigmoid(gg))
    mix = jnp.concatenate([odn, ogn], axis=-1).astype(bf16)
    y = x_ref[...] + jnp.dot(mix, wo_ref[...], preferred_element_type=f32)
    n2 = (y * lax.rsqrt(jnp.mean(y * y, axis=-1, keepdims=True) + EPS) * n2_ref[...]).astype(bf16)
    return y, n2


def _ffn_chunk(n2, wgu_ref, wd_ref, c0, ff_chunk, d_ff):
    a = jnp.dot(n2, wgu_ref[:, c0:c0 + ff_chunk], preferred_element_type=f32)
    b = jnp.dot(n2, wgu_ref[:, d_ff + c0:d_ff + c0 + ff_chunk], preferred_element_type=f32)
    hid = (a * jax.nn.sigmoid(a) * b).astype(bf16)
    return jnp.dot(hid, wd_ref[c0:c0 + ff_chunk, :], preferred_element_type=f32)


def _post_kernel(x_ref, od_ref, og_ref, gg_ref, sub_ref, gn_ref, wo_ref, n2_ref, wgu_ref, wd_ref, y_ref,
                 *, lam_init, d_ff, ff_chunk):
    acc, n2 = _merge_ffn_pieces(x_ref, od_ref, og_ref, gg_ref, sub_ref, gn_ref, wo_ref, n2_ref, lam_init)
    for c0 in range(0, d_ff, ff_chunk):
        acc = acc + _ffn_chunk(n2, wgu_ref, wd_ref, c0, ff_chunk, d_ff)
    y_ref[...] = acc


def _post(x2d, od, og, gg, sub, gn, w_out, n2, w_gu, w_down, *, tm, lam_init):
    n, d = x2d.shape
    d_ff = w_down.shape[0]
    row = lambda w: pl.BlockSpec((tm, w), lambda i: (i, 0))
    return pl.pallas_call(
        functools.partial(_post_kernel, lam_init=lam_init, d_ff=d_ff, ff_chunk=2 * LANES),
        grid=(n // tm,),
        in_specs=[row(d), row(DV_W), row(GV_W), row(GV_W), _const_spec((1, DV_W)), _const_spec((1, GV_W)),
                  _const_spec(w_out.shape), _const_spec((1, d)), _const_spec(w_gu.shape), _const_spec(w_down.shape)],
        out_specs=row(d),
        out_shape=jax.ShapeDtypeStruct((n, d), f32),
        compiler_params=_cparams(("parallel",)),
        name="post",
    )(x2d, od, og, gg, sub, gn, w_out, n2, w_gu, w_down)


def _sample_attention(q, k_new, v_new, lam, kbuf, vbuf, slot, *, n_pages, page):
    past = n_pages * page
    nrow = 2 * DIFF_HEADS
    qb = q.astype(bf16)
    rid = lax.broadcasted_iota(jnp.int32, (nrow, past), 0)
    s = jnp.zeros((nrow, past), f32)
    for h in range(DIFF_HEADS):
        for mp in range(2):
            kt = jnp.concatenate([kbuf[slot, j, h, mp] for j in range(n_pages)], axis=-1)
            sr = jnp.dot(qb, kt.astype(bf16), preferred_element_type=f32)
            s = jnp.where(rid == 2 * h + mp, sr, s)
    s_new = jnp.sum(q * k_new, axis=-1, keepdims=True)
    m = jnp.maximum(jnp.max(s, axis=-1, keepdims=True), s_new)
    p = jnp.exp2(s - m)
    p_new = jnp.exp2(s_new - m)
    inv_l = 1.0 / (jnp.sum(p, axis=-1, keepdims=True) + p_new)
    pb = p.astype(bf16)
    heads = []
    for h in range(DIFF_HEADS):
        vh = jnp.concatenate([vbuf[slot, j, pl.ds(h, page, stride=DIFF_HEADS), :] for j in range(n_pages)], axis=0)
        oh = jnp.dot(pb, vh.astype(bf16), preferred_element_type=f32)
        oh = (oh + p_new * v_new[:, h * DIFF_DV:(h + 1) * DIFF_DV]) * inv_l
        heads.append(oh[2 * h:2 * h + 1, :] - lam * oh[2 * h + 1:2 * h + 2, :])
    return jnp.concatenate(heads, axis=-1)


def _page_copies(pt_ref, ck_hbm, cv_hbm, kbuf, vbuf, sem, sample, slot, *, layer, n_pages):
    out = []
    for j in range(n_pages):
        phys = pt_ref[sample * n_pages + j]
        out.append(pltpu.make_async_copy(ck_hbm.at[layer, phys], kbuf.at[slot, j], sem.at[0, slot]))
        out.append(pltpu.make_async_copy(cv_hbm.at[layer, phys], vbuf.at[slot, j], sem.at[1, slot]))
    return out


def _post_attn_kernel(pt_ref, x_ref, od_ref, og_ref, gg_ref, sub_ref, gn_ref, wo_ref, n2_ref, wgu_ref, wd_ref,
                      q_ref, kn_ref, vn_ref, lqk_ref, ck_hbm, cv_hbm, y_ref, os_ref, kbuf, vbuf, sem,
                      *, lam_init, d_ff, ff_chunk, layer, n_pages, page, per_step):
    step = pl.program_id(0)
    nsteps = pl.num_programs(0)
    copies = functools.partial(_page_copies, pt_ref, ck_hbm, cv_hbm, kbuf, vbuf, sem, layer=layer, n_pages=n_pages)

    @pl.when(step == 0)
    def _():
        for cp in copies(0, 0):
            cp.start()

    lam = _diff_lambda(lqk_ref[...], lam_init)

    def sample(j):
        g = step * per_step + j
        slot = j % 2
        if j + 1 < per_step:
            for cp in copies(g + 1, 1 - slot):
                cp.start()
        else:
            @pl.when(step + 1 < nsteps)
            def _():
                for cp in copies(g + 1, 1 - slot):
                    cp.start()
        for cp in copies(g, slot):
            cp.wait()
        os_ref[j] = _sample_attention(q_ref[j], kn_ref[j], vn_ref[j], lam, kbuf, vbuf, slot,
                                      n_pages=n_pages, page=page)

    acc, n2 = _merge_ffn_pieces(x_ref, od_ref, og_ref, gg_ref, sub_ref, gn_ref, wo_ref, n2_ref, lam_init)
    nchunk = d_ff // ff_chunk
    sample_at = {(j * nchunk) // per_step: j for j in range(per_step)}
    for ic in range(nchunk):
        if ic in sample_at:
            sample(sample_at[ic])
        acc = acc + _ffn_chunk(n2, wgu_ref, wd_ref, ic * ff_chunk, ff_chunk, d_ff)
    y_ref[...] = acc


def _post_attn(x2d, od, og, gg, sub, gn, w_out, n2, w_gu, w_down, page_table, q_s, kn_s, vn_s, lqk, cache_kt, cache_vr,
               *, tm, lam_init, layer):
    n, d = x2d.shape
    d_ff = w_down.shape[0]
    nsteps = n // tm
    nb, n_pages = page_table.shape
    per_step = nb // nsteps
    assert per_step * nsteps == nb and per_step % 2 == 0
    page = cache_kt.shape[-1]
    nrow = 2 * DIFF_HEADS
    row = lambda w: pl.BlockSpec((tm, w), lambda i, pt: (i, 0))
    const = lambda shape: pl.BlockSpec(shape, lambda i, pt: (0,) * len(shape), pipeline_mode=pl.Buffered(1))
    per = lambda r, w: pl.BlockSpec((per_step, r, w), lambda i, pt: (i, 0, 0))
    grid_spec = pltpu.PrefetchScalarGridSpec(
        num_scalar_prefetch=1,
        grid=(nsteps,),
        in_specs=[row(d), row(DV_W), row(GV_W), row(GV_W), const((1, DV_W)), const((1, GV_W)), const(w_out.shape),
                  const((1, d)), const(w_gu.shape), const(w_down.shape),
                  per(nrow, DIFF_DK), per(nrow, DIFF_DK), per(1, DV_W), const(lqk.shape),
                  pl.BlockSpec(memory_space=pl.ANY), pl.BlockSpec(memory_space=pl.ANY)],
        out_specs=(row(d), per(1, DV_W)),
        scratch_shapes=[pltpu.VMEM((2, n_pages) + cache_kt.shape[2:], f32),
                        pltpu.VMEM((2, n_pages) + cache_vr.shape[2:], f32),
                        pltpu.SemaphoreType.DMA((2, 2))],
    )
    y, o_s = pl.pallas_call(
        functools.partial(_post_attn_kernel, lam_init=lam_init, d_ff=d_ff, ff_chunk=2 * LANES, layer=layer,
                          n_pages=n_pages, page=page, per_step=per_step),
        grid_spec=grid_spec,
        out_shape=(jax.ShapeDtypeStruct((n, d), f32), jax.ShapeDtypeStruct((nb, 1, DV_W), f32)),
        compiler_params=pltpu.CompilerParams(dimension_semantics=("arbitrary",), vmem_limit_bytes=FUSED_VMEM_LIMIT),
        name="post_attn",
    )(page_table.reshape(-1), x2d, od, og, gg, sub, gn, w_out, n2, w_gu, w_down,
      q_s.reshape(nb, nrow, DIFF_DK), kn_s.reshape(nb, nrow, DIFF_DK), vn_s.reshape(nb, 1, DV_W), lqk,
      cache_kt, cache_vr)
    return y, o_s.reshape(nb, DV_W)


def _gla_sample_kernel(qt_ref, kt_ref, v_ref, et_ref, s0_ref, o_ref, s_ref, *, tb):
    qt = qt_ref[0]
    kt = kt_ref[0]
    et = et_ref[0]
    v = v_ref[...]
    for i in range(tb):
        for h in range(GLA_HEADS):
            ks = slice(h * GLA_DK, (h + 1) * GLA_DK)
            vs = slice(h * GLA_DV, (h + 1) * GLA_DV)
            s_new = s0_ref[i, h] * et[ks, i:i + 1] + kt[ks, i:i + 1] * v[i:i + 1, vs]
            s_ref[i, h] = s_new
            o_ref[i:i + 1, vs] = jnp.sum(qt[ks, i:i + 1] * s_new, axis=0, keepdims=True)


def _gla_sample(gqt, gkt, gv, et, s0):
    nblk, _, tb = gqt.shape
    tr = pl.BlockSpec((1, GK_W, tb), lambda i: (i, 0, 0))
    row = pl.BlockSpec((tb, GV_W), lambda i: (i, 0))
    st = pl.BlockSpec((tb, GLA_HEADS, GLA_DK, GLA_DV), lambda i: (i, 0, 0, 0))
    return pl.pallas_call(
        functools.partial(_gla_sample_kernel, tb=tb),
        grid=(nblk,),
        in_specs=[tr, tr, row, tr, st],
        out_specs=(row, st),
        out_shape=(jax.ShapeDtypeStruct((nblk * tb, GV_W), f32), jax.ShapeDtypeStruct(s0.shape, f32)),
        compiler_params=_cparams(("parallel",)),
        name="gla_sample",
    )(gqt, gkt, gv, et, s0)


def _rope_tables(pos):
    half = DIFF_DK // 2
    freqs = ROPE_THETA ** (-jnp.arange(half, dtype=f32) / half)
    ang = pos.astype(f32)[:, None] * freqs[None, :]
    cos, sin = jnp.cos(ang), jnp.sin(ang)
    reps = LANES // DIFF_DK
    return jnp.tile(jnp.concatenate([cos, cos], -1), (1, reps)), jnp.tile(jnp.concatenate([-sin, sin], -1), (1, reps))


def kernel(x_prompt, x_sample, cache_k, cache_v, state_gla, page_table, norm1, w_in, q_norm, k_norm, lambda_qk,
           subln, w_a2, b_a, gla_norm, w_out, norm2, w_gu, w_down):
    batch, seq, d = x_prompt.shape
    nb, dec_seq, _ = x_sample.shape
    assert dec_seq == 1
    depth = w_in.shape[0]
    n_pages = page_table.shape[1]
    page = cache_k.shape[2]
    past = n_pages * page
    rank = w_a2.shape[1]

    cos_p, sin_p = _rope_tables(jnp.arange(seq))
    cos_s, sin_s = _rope_tables(past + jnp.arange(dec_seq))
    cos_s = jnp.broadcast_to(cos_s, (nb, LANES))
    sin_s = jnp.broadcast_to(sin_s, (nb, LANES))

    ck = jnp.transpose(cache_k, (0, 1, 3, 4, 5, 2))
    cv = cache_v.reshape(depth, -1, page * DIFF_HEADS, DIFF_DV)

    yp = x_prompt.reshape(batch * seq, d)
    ys = x_sample.reshape(nb, d)
    kp, vp, sp, k_s, v_s, s_s = [], [], [], [], [], []
    for l in range(depth):
        lam_init = 0.8 - 0.6 * math.exp(-0.3 * l)
        w_main = w_in[l, :, :MAIN_W].astype(bf16)
        w_r = jnp.pad(w_in[l, :, MAIN_W:], ((0, 0), (0, LANES - rank))).astype(bf16)
        wa2 = jnp.pad(w_a2[l], ((0, LANES - rank), (0, 0))).astype(bf16)
        proj_args = (norm1[l][None], w_main, w_r, wa2, b_a[l][None],
                     jnp.tile(q_norm[l], QK_W // DIFF_DK)[None], jnp.tile(k_norm[l], QK_W // DIFF_DK)[None])
        post_args = (jnp.tile(subln[l], DIFF_HEADS)[None], jnp.tile(gla_norm[l], GLA_HEADS)[None],
                     w_out[l].astype(bf16), norm2[l][None], w_gu[l].astype(bf16), w_down[l].astype(bf16))

        q, kb, kt, vb, v4, gq, gk, gv, gl, gg = _proj_prompt(yp, *proj_args, cos_p, sin_p, batch=batch, tm=512)
        q_s, k_s_l, v_s_l, gqt, gkt, gv_s, et, gg_s = _proj_sample(ys, *proj_args, cos_s, sin_s, tb=GLA_SAMPLE_BLOCK)
        od = _attn_prompt(q, kb, vb, lambda_qk[l], batch=batch, seq=seq, tq=512, lam_init=lam_init)
        og, s_fin = _gla_prompt(gq, gk, gv, gl, batch=batch, seq=seq, tg=512)
        yp, od_s = _post_attn(yp, od, og, gg, *post_args, page_table, q_s, k_s_l, v_s_l, lambda_qk[l], ck, cv,
                              tm=512, lam_init=lam_init, layer=l)
        kp.append(kt)
        vp.append(v4)
        sp.append(s_fin)

        og_s, s_new = _gla_sample(gqt, gkt, gv_s, et, state_gla[l])
        ys = _post(ys, od_s, og_s, gg_s, *post_args, tm=nb, lam_init=lam_init)
        k_s.append(k_s_l.reshape(nb, dec_seq, DIFF_HEADS, 2, DIFF_DK))
        v_s.append(v_s_l.reshape(nb, dec_seq, DIFF_HEADS, DIFF_DV))
        s_s.append(s_new)

    k_prompt = jnp.stack(kp).reshape(depth, batch, DIFF_HEADS, 2, DIFF_DK, seq).transpose(0, 1, 5, 2, 3, 4)
    v_prompt = jnp.stack(vp).reshape(depth, batch, seq, DIFF_HEADS, DIFF_DV)
    return (yp.reshape(batch, seq, d), ys.reshape(nb, dec_seq, d), k_prompt, v_prompt, jnp.stack(sp),
            jnp.stack(k_s), jnp.stack(v_s), jnp.stack(s_s))
```

```python
import functools
import math

import jax
import jax.numpy as jnp
from jax import lax
from jax.experimental import pallas as pl
from jax.experimental.pallas import tpu as pltpu

f32 = jnp.float32
bf16 = jnp.bfloat16

DIFF_HEADS = 4
DIFF_DK = 64
DIFF_DV = 128
GLA_HEADS = 4
GLA_DK = 64
GLA_DV = 128
GLA_GATE_NORM = 16.0
ROPE_THETA = 10000.0
EPS = 1e-6
LOG2_E = math.log2(math.e)

QK_W = DIFF_HEADS * 2 * DIFF_DK
DV_W = DIFF_HEADS * DIFF_DV
GK_W = GLA_HEADS * GLA_DK
GV_W = GLA_HEADS * GLA_DV
MAIN_W = 2 * QK_W + DV_W + 2 * GK_W + 2 * GV_W

LANES = 128
GLA_SAMPLE_BLOCK = 32
GLA_BLOCK = 256
GLA_SAFE_LOG_DECAY = 80.0
VMEM_LIMIT = 56 * 1024 * 1024
FUSED_VMEM_LIMIT = 60 * 1024 * 1024


def _cparams(sem):
    return pltpu.CompilerParams(dimension_semantics=sem, vmem_limit_bytes=VMEM_LIMIT)


def _const_spec(shape):
    nd = len(shape)
    return pl.BlockSpec(shape, lambda *_: (0,) * nd, pipeline_mode=pl.Buffered(1))


def _layer_spec(shape, layer):
    nd = len(shape)
    return pl.BlockSpec((1,) + tuple(shape[1:]), lambda *_: (layer,) + (0,) * (nd - 1), pipeline_mode=pl.Buffered(1))


def _group_scale(z, width):
    lane = lax.broadcasted_iota(jnp.int32, (z.shape[0], LANES), 1)
    cols = []
    for c in range(z.shape[1] // LANES):
        zc = z[:, c * LANES:(c + 1) * LANES]
        zz = zc * zc
        if width == LANES:
            cols.append(jnp.broadcast_to(lax.rsqrt(jnp.mean(zz, axis=-1, keepdims=True) + EPS), zc.shape))
        else:
            lo = lane < width
            s_lo = jnp.sum(jnp.where(lo, zz, 0.0), axis=-1, keepdims=True)
            s_hi = jnp.sum(jnp.where(lo, 0.0, zz), axis=-1, keepdims=True)
            r_lo = lax.rsqrt(s_lo * (1.0 / width) + EPS)
            r_hi = lax.rsqrt(s_hi * (1.0 / width) + EPS)
            cols.append(jnp.where(lo, r_lo, r_hi))
    return jnp.concatenate(cols, axis=-1)


def _rope(z, cos, sin_signed):
    lane = lax.broadcasted_iota(jnp.int32, (z.shape[0], LANES), 1)
    first_half = (lane % DIFF_DK) < (DIFF_DK // 2)
    cols = []
    for c in range(z.shape[1] // LANES):
        zc = z[:, c * LANES:(c + 1) * LANES]
        partner = jnp.where(first_half, pltpu.roll(zc, LANES - DIFF_DK // 2, 1), pltpu.roll(zc, DIFF_DK // 2, 1))
        cols.append(zc * cos + partner * sin_signed)
    return jnp.concatenate(cols, axis=-1)


def _diff_lambda(lqk, lam_init):
    a = jnp.sum(lqk[0:1, :] * lqk[1:2, :], axis=-1, keepdims=True)
    b = jnp.sum(lqk[2:3, :] * lqk[3:4, :], axis=-1, keepdims=True)
    return jnp.exp(a) - jnp.exp(b) + lam_init


def _proj_body(x, g1_ref, w_ref, wr_ref, wa2_ref, ba_ref, qn_ref, kn_ref, cos, sin):
    n = x * lax.rsqrt(jnp.mean(x * x, axis=-1, keepdims=True) + EPS) * g1_ref[...]
    nb = n.astype(bf16)

    def seg(lo, width):
        return jnp.dot(nb, w_ref[0, :, lo:lo + width], preferred_element_type=f32)

    zq = seg(0, QK_W)
    q = _rope(zq * _group_scale(zq, DIFF_DK) * qn_ref[...], cos, sin) * (DIFF_DK ** -0.5 * LOG2_E)
    zk = seg(QK_W, QK_W)
    k = _rope(zk * _group_scale(zk, DIFF_DK) * kn_ref[...], cos, sin)
    v = seg(2 * QK_W, DV_W)
    off = 2 * QK_W + DV_W
    gq = seg(off, GK_W) * (GLA_DK ** -0.5)
    gk = seg(off + GK_W, GK_W)
    gv = seg(off + 2 * GK_W, GV_W)
    gg = seg(off + 2 * GK_W + GV_W, GV_W)
    r = jnp.dot(nb, wr_ref[0], preferred_element_type=f32)
    a = jnp.dot(r.astype(bf16), wa2_ref[0], preferred_element_type=f32) + ba_ref[...]
    gl = (jnp.minimum(a, 0.0) - jnp.log1p(jnp.exp(-jnp.abs(a)))) * (1.0 / GLA_GATE_NORM)
    return q, k, v, gq, gk, gv, gl, gg


def _proj_prompt_kernel(x_ref, g1_ref, w_ref, wr_ref, wa2_ref, ba_ref, qn_ref, kn_ref, cos_ref, sin_ref,
                        q_ref, kb_ref, kt_ref, vb_ref, v4_ref, gq_ref, gk_ref, gv_ref, gl_ref, gg_ref):
    tm = x_ref.shape[0]
    q, k, v, gq, gk, gv, gl, gg = _proj_body(x_ref[...], g1_ref, w_ref, wr_ref, wa2_ref, ba_ref, qn_ref, kn_ref,
                                             cos_ref[...], sin_ref[...])
    first_map = lax.broadcasted_iota(jnp.int32, (tm, LANES), 1) < DIFF_DK
    for h in range(DIFF_HEADS):
        qh = q[:, h * LANES:(h + 1) * LANES]
        q_ref[:, (2 * h) * LANES:(2 * h + 1) * LANES] = jnp.where(first_map, qh, 0.0).astype(bf16)
        q_ref[:, (2 * h + 1) * LANES:(2 * h + 2) * LANES] = jnp.where(first_map, 0.0, qh).astype(bf16)
    kb_ref[...] = k.astype(bf16)
    kt_ref[0] = k.T
    vb_ref[...] = v.astype(bf16)
    for h in range(DIFF_HEADS):
        v4_ref[pl.ds(h, tm, stride=DIFF_HEADS), :] = v[:, h * DIFF_DV:(h + 1) * DIFF_DV]
    gq_ref[...] = gq
    gk_ref[...] = gk
    gv_ref[...] = gv.astype(bf16)
    gl_ref[...] = gl
    gg_ref[...] = gg


def _proj_prompt(x2d, g1, w_main, w_r, w_a2, b_a, qn, kn, cos_t, sin_t, *, layer, batch, tm):
    n, d = x2d.shape
    seq = n // batch
    nt = seq // tm
    row = lambda w: pl.BlockSpec((tm, w), lambda b, j: (b * nt + j, 0))
    tab = pl.BlockSpec((tm, LANES), lambda b, j: (j, 0))
    arr = lambda w, dt: jax.ShapeDtypeStruct((n, w), dt)
    out_shape = (
        arr(2 * QK_W, bf16),
        arr(QK_W, bf16),
        jax.ShapeDtypeStruct((batch, QK_W, seq), f32),
        arr(DV_W, bf16),
        jax.ShapeDtypeStruct((n * DIFF_HEADS, DIFF_DV), f32),
        arr(GK_W, f32), arr(GK_W, f32), arr(GV_W, bf16), arr(GK_W, f32), arr(GV_W, f32),
    )
    out_specs = (row(2 * QK_W), row(QK_W), pl.BlockSpec((1, QK_W, tm), lambda b, j: (b, 0, j)), row(DV_W),
                 pl.BlockSpec((tm * DIFF_HEADS, DIFF_DV), lambda b, j: (b * nt + j, 0)),
                 row(GK_W), row(GK_W), row(GV_W), row(GK_W), row(GV_W))
    return pl.pallas_call(
        _proj_prompt_kernel,
        grid=(batch, nt),
        in_specs=[row(d), _const_spec((1, d)), _layer_spec(w_main.shape, layer), _layer_spec(w_r.shape, layer),
                  _layer_spec(w_a2.shape, layer), _const_spec((1, GK_W)), _const_spec((1, QK_W)),
                  _const_spec((1, QK_W)), tab, tab],
        out_specs=out_specs,
        out_shape=out_shape,
        compiler_params=_cparams(("parallel", "parallel")),
        name="proj_prompt",
    )(x2d, g1, w_main, w_r, w_a2, b_a, qn, kn, cos_t, sin_t)


def _proj_sample_kernel(x_ref, g1_ref, w_ref, wr_ref, wa2_ref, ba_ref, qn_ref, kn_ref, cos_ref, sin_ref,
                        q_ref, k_ref, v_ref, gqt_ref, gkt_ref, gv_ref, et_ref, gg_ref):
    q, k, v, gq, gk, gv, gl, gg = _proj_body(x_ref[...], g1_ref, w_ref, wr_ref, wa2_ref, ba_ref, qn_ref, kn_ref,
                                             cos_ref[...], sin_ref[...])
    q_ref[...] = q
    k_ref[...] = k
    v_ref[...] = v
    gv_ref[...] = gv
    gg_ref[...] = gg
    e = jnp.exp(gl)
    tb = gqt_ref.shape[2]
    for blk in range(gqt_ref.shape[0]):
        rows = slice(blk * tb, (blk + 1) * tb)
        gqt_ref[blk] = gq[rows, :].T
        gkt_ref[blk] = gk[rows, :].T
        et_ref[blk] = e[rows, :].T


def _proj_sample(x2d, g1, w_main, w_r, w_a2, b_a, qn, kn, cos_t, sin_t, *, layer, tb):
    n, d = x2d.shape
    flat = lambda w: jax.ShapeDtypeStruct((n, w), f32)
    tr = jax.ShapeDtypeStruct((n // tb, GK_W, tb), f32)
    out_shape = (flat(QK_W), flat(QK_W), flat(DV_W), tr, tr, flat(GV_W), tr, flat(GV_W))
    whole = lambda shape: pl.BlockSpec(shape, lambda i: (0,) * len(shape))
    return pl.pallas_call(
        _proj_sample_kernel,
        grid=(1,),
        in_specs=[whole(x2d.shape), whole((1, d)), _layer_spec(w_main.shape, layer), _layer_spec(w_r.shape, layer),
                  _layer_spec(w_a2.shape, layer), whole((1, GK_W)), whole((1, QK_W)), whole((1, QK_W)),
                  whole(cos_t.shape), whole(sin_t.shape)],
        out_specs=tuple(whole(o.shape) for o in out_shape),
        out_shape=out_shape,
        compiler_params=_cparams(("arbitrary",)),
        name="proj_sample",
    )(x2d, g1, w_main, w_r, w_a2, b_a, qn, kn, cos_t, sin_t)


def _attn_kernel(q_ref, kb_ref, v_ref, lqk_ref, o_ref, vb_ref, s_ref, *, tq, lam_init):
    seq = kb_ref.shape[0]
    vb_ref[:, :DIFF_DV] = v_ref[...]
    vb_ref[:, DIFF_DV:] = jnp.ones((seq, LANES), bf16)

    lam = _diff_lambda(lqk_ref[...], lam_init)
    row = lax.broadcasted_iota(jnp.int32, (2 * tq, tq), 0) % tq
    col = lax.broadcasted_iota(jnp.int32, (2 * tq, tq), 1)
    visible = col <= row
    for qi in range(seq // tq):
        rows = slice(qi * tq, (qi + 1) * tq)
        qs = jnp.concatenate([q_ref[rows, :LANES], q_ref[rows, LANES:]], axis=0)
        mx = None
        for kt in range(qi + 1):
            cols = slice(kt * tq, (kt + 1) * tq)
            s = lax.dot_general(qs, kb_ref[cols, :], (((1,), (1,)), ((), ())), preferred_element_type=f32)
            if kt == qi:
                s = jnp.where(visible, s, -jnp.inf)
            s_ref[:, cols] = s
            for c in range(tq // LANES):
                sc = s[:, c * LANES:(c + 1) * LANES]
                mx = sc if mx is None else jnp.maximum(mx, sc)
        m = jnp.broadcast_to(jnp.max(mx, axis=-1, keepdims=True), (2 * tq, tq))
        acc = None
        for kt in range(qi + 1):
            cols = slice(kt * tq, (kt + 1) * tq)
            p = jnp.exp2(s_ref[:, cols] - m).astype(bf16)
            pv = jnp.dot(p, vb_ref[cols, :], preferred_element_type=f32)
            acc = pv if acc is None else acc + pv
        o1 = acc[:tq, :DIFF_DV] / acc[:tq, DIFF_DV:]
        o2 = acc[tq:, :DIFF_DV] / acc[tq:, DIFF_DV:]
        o_ref[rows, :] = o1 - lam * o2


def _attn_prompt(q, kb, vb, lqk, *, batch, seq, tq, lam_init):
    return pl.pallas_call(
        functools.partial(_attn_kernel, tq=tq, lam_init=lam_init),
        grid=(batch, DIFF_HEADS),
        in_specs=[pl.BlockSpec((seq, 2 * LANES), lambda b, h: (b, h)),
                  pl.BlockSpec((seq, LANES), lambda b, h: (b, h)),
                  pl.BlockSpec((seq, LANES), lambda b, h: (b, h)),
                  _const_spec(lqk.shape)],
        out_specs=pl.BlockSpec((seq, DIFF_DV), lambda b, h: (b, h)),
        out_shape=jax.ShapeDtypeStruct((batch * seq, DV_W), f32),
        scratch_shapes=[pltpu.VMEM((seq, 2 * LANES), bf16), pltpu.VMEM((2 * tq, seq), f32)],
        compiler_params=_cparams(("parallel", "parallel")),
        name="attn_prompt",
    )(q, kb, vb, lqk)


def _cumsum_rows(g, tril):
    g1 = g.astype(bf16)
    r1 = g - g1.astype(f32)
    g2 = r1.astype(bf16)
    g3 = (r1 - g2.astype(f32)).astype(bf16)
    dot = lambda t: jnp.dot(tril, t, preferred_element_type=f32)
    return dot(g1) + dot(g2) + dot(g3)


def _gla_kernel(q_ref, k_ref, v_ref, g_ref, o_ref, s_ref, st_ref, b_ref, *, tg):
    step = pl.program_id(1)

    @pl.when(step == 0)
    def _():
        st_ref[...] = jnp.zeros(st_ref.shape, f32)

    c = GLA_BLOCK
    nblock = tg // c
    ri = lax.broadcasted_iota(jnp.int32, (c, c), 0)
    ci = lax.broadcasted_iota(jnp.int32, (c, c), 1)
    causal = ri >= ci
    tril = jnp.where(causal, 1.0, 0.0).astype(bf16)
    head0 = lax.broadcasted_iota(jnp.int32, (c, LANES), 1) < GLA_DK
    lane_sq = lax.broadcasted_iota(jnp.int32, (LANES, LANES), 1) < GLA_DK
    block_rows = lambda ib: slice(ib * c, (ib + 1) * c)
    head_cols = lambda h: slice(h * GLA_DV, (h + 1) * GLA_DV)
    head_mask = lambda j: head0 if j == 0 else jnp.logical_not(head0)

    worst = None
    for ib in range(nblock):
        b = _cumsum_rows(g_ref[block_rows(ib), :], tril)
        b_ref[block_rows(ib), :] = b
        b_mid = b[c // 2 - 1:c // 2, :]
        w = jnp.maximum(-b_mid, b_mid - b[c - 1:c, :])
        worst = w if worst is None else jnp.maximum(worst, w)
    safe = jnp.max(worst) < GLA_SAFE_LOG_DECAY

    @pl.when(safe)
    def _():
        for ib in range(nblock):
            rows = block_rows(ib)
            b = b_ref[rows, :]
            b_mid = b[c // 2 - 1:c // 2, :]
            qe = q_ref[rows, :] * jnp.exp(b - b_mid)
            ke = (k_ref[rows, :] * jnp.exp(b_mid - b)).astype(bf16)
            for h in range(GLA_HEADS):
                sl = slice((h // 2) * LANES, (h // 2 + 1) * LANES)
                qe_h = jnp.where(head_mask(h % 2), qe[:, sl], 0.0).astype(bf16)
                a = lax.dot_general(qe_h, ke[:, sl], (((1,), (1,)), ((), ())), preferred_element_type=f32)
                a = jnp.where(causal, a, 0.0).astype(bf16)
                o_ref[rows, head_cols(h)] = jnp.dot(a, v_ref[rows, head_cols(h)], preferred_element_type=f32)

    @pl.when(jnp.logical_not(safe))
    def _():
        tok = lax.broadcasted_iota(jnp.int32, (c, 1), 0)

        def exact_block(ib, carry):
            rows = pl.ds(pl.multiple_of(ib * c, c), c)
            b = b_ref[rows, :]
            q = q_ref[rows, :]
            k = k_ref[rows, :]
            v = v_ref[rows, :].astype(f32)
            pick = lambda x, s: jnp.sum(jnp.where(tok == s, x, 0.0), axis=0, keepdims=True)

            def key(s, acc):
                w = q * pick(k, s) * jnp.exp(jnp.minimum(b - pick(b, s), 0.0))
                w = jnp.where(tok >= s, w, 0.0)
                vs = pick(v, s)
                return tuple(
                    acc[h] + jnp.sum(w[:, h * GLA_DK:(h + 1) * GLA_DK], axis=-1, keepdims=True) * vs[:, head_cols(h)]
                    for h in range(GLA_HEADS))

            acc = lax.fori_loop(0, c, key, tuple(jnp.zeros((c, GLA_DV), f32) for _ in range(GLA_HEADS)))
            for h in range(GLA_HEADS):
                o_ref[rows, head_cols(h)] = acc[h]
            return carry

        lax.fori_loop(0, nblock, exact_block, 0)

    for ib in range(nblock):
        rows = block_rows(ib)
        b = b_ref[rows, :]
        b_last = b[c - 1:c, :]
        qb = q_ref[rows, :] * jnp.exp(b)
        kd = (k_ref[rows, :] * jnp.exp(b_last - b)).astype(bf16)
        e_last = jnp.exp(b_last)
        for p in range(GLA_HEADS // 2):
            sl = slice(p * LANES, (p + 1) * LANES)
            st = st_ref[p]
            stb = st.astype(bf16)
            upd = []
            for j in range(2):
                h = 2 * p + j
                qb_h = jnp.where(head_mask(j), qb[:, sl], 0.0).astype(bf16)
                o_ref[rows, head_cols(h)] += lax.dot_general(qb_h, stb, (((1,), (1,)), ((), ())),
                                                             preferred_element_type=f32)
                upd.append(lax.dot_general(v_ref[rows, head_cols(h)], kd[:, sl], (((0,), (0,)), ((), ())),
                                           preferred_element_type=f32))
            st_ref[p] = st * e_last[:, sl] + jnp.where(lane_sq, upd[0], upd[1])

    @pl.when(step == pl.num_programs(1) - 1)
    def _():
        for p in range(GLA_HEADS // 2):
            t = st_ref[p].T
            s_ref[0, 2 * p] = t[:GLA_DK, :]
            s_ref[0, 2 * p + 1] = t[GLA_DK:, :]


def _gla_prompt(gq, gk, gv, gl, *, batch, seq, tg):
    ns = seq // tg
    row = lambda w: pl.BlockSpec((tg, w), lambda b, i: (b * ns + i, 0))
    return pl.pallas_call(
        functools.partial(_gla_kernel, tg=tg),
        grid=(batch, ns),
        in_specs=[row(GK_W), row(GK_W), row(GV_W), row(GK_W)],
        out_specs=(row(GV_W), pl.BlockSpec((1, GLA_HEADS, GLA_DK, GLA_DV), lambda b, i: (b, 0, 0, 0))),
        out_shape=(jax.ShapeDtypeStruct((batch * seq, GV_W), f32),
                   jax.ShapeDtypeStruct((batch, GLA_HEADS, GLA_DK, GLA_DV), f32)),
        scratch_shapes=[pltpu.VMEM((GLA_HEADS // 2, LANES, LANES), f32), pltpu.VMEM((tg, GK_W), f32)],
        compiler_params=_cparams(("parallel", "arbitrary")),
        name="gla_prompt",
    )(gq, gk, gv, gl)


def _merge_ffn_pieces(x_ref, od_ref, og_ref, gg_ref, sub_ref, gn_ref, wo_ref, n2_ref, lam_init):
    od = od_ref[...]
    odn = od * _group_scale(od, DIFF_DV) * sub_ref[...] * (1.0 - lam_init)
    og = og_ref[...]
    ogn = og * _group_scale(og, GLA_DV) * gn_ref[...]
    gg = gg_ref[...]
    ogn = ogn * (gg * jax.nn.sigmoid(gg))
    mix = jnp.concatenate([odn, ogn], axis=-1).astype(bf16)
    y = x_ref[...] + jnp.dot(mix, wo_ref[0], preferred_element_type=f32)
    n2 = (y * lax.rsqrt(jnp.mean(y * y, axis=-1, keepdims=True) + EPS) * n2_ref[...]).astype(bf16)
    return y, n2


def _ffn_chunk(n2, wgu_ref, wd_ref, c0, ff_chunk, d_ff):
    a = jnp.dot(n2, wgu_ref[0, :, c0:c0 + ff_chunk], preferred_element_type=f32)
    b = jnp.dot(n2, wgu_ref[0, :, d_ff + c0:d_ff + c0 + ff_chunk], preferred_element_type=f32)
    hid = (a * jax.nn.sigmoid(a) * b).astype(bf16)
    return jnp.dot(hid, wd_ref[0, c0:c0 + ff_chunk, :], preferred_element_type=f32)


def _post_kernel(x_ref, od_ref, og_ref, gg_ref, sub_ref, gn_ref, wo_ref, n2_ref, wgu_ref, wd_ref, y_ref,
                 *, lam_init, d_ff, ff_chunk):
    acc, n2 = _merge_ffn_pieces(x_ref, od_ref, og_ref, gg_ref, sub_ref, gn_ref, wo_ref, n2_ref, lam_init)
    for c0 in range(0, d_ff, ff_chunk):
        acc = acc + _ffn_chunk(n2, wgu_ref, wd_ref, c0, ff_chunk, d_ff)
    y_ref[...] = acc


def _post(x2d, od, og, gg, sub, gn, w_out, n2, w_gu, w_down, *, layer, tm, lam_init):
    n, d = x2d.shape
    d_ff = w_down.shape[1]
    row = lambda w: pl.BlockSpec((tm, w), lambda i: (i, 0))
    return pl.pallas_call(
        functools.partial(_post_kernel, lam_init=lam_init, d_ff=d_ff, ff_chunk=2 * LANES),
        grid=(n // tm,),
        in_specs=[row(d), row(DV_W), row(GV_W), row(GV_W), _const_spec((1, DV_W)), _const_spec((1, GV_W)),
                  _layer_spec(w_out.shape, layer), _const_spec((1, d)), _layer_spec(w_gu.shape, layer),
                  _layer_spec(w_down.shape, layer)],
        out_specs=row(d),
        out_shape=jax.ShapeDtypeStruct((n, d), f32),
        compiler_params=_cparams(("parallel",)),
        name="post",
    )(x2d, od, og, gg, sub, gn, w_out, n2, w_gu, w_down)


def _sample_attention(q, k_new, v_new, lam, kbuf, vbuf, slot, *, n_pages, page):
    past = n_pages * page
    nrow = 2 * DIFF_HEADS
    qb = q.astype(bf16)
    rid = lax.broadcasted_iota(jnp.int32, (nrow, past), 0)
    s = jnp.zeros((nrow, past), f32)
    for h in range(DIFF_HEADS):
        for mp in range(2):
            kt = jnp.concatenate([kbuf[slot, j, h, mp] for j in range(n_pages)], axis=-1)
            sr = jnp.dot(qb, kt.astype(bf16), preferred_element_type=f32)
            s = jnp.where(rid == 2 * h + mp, sr, s)
    s_new = jnp.sum(q * k_new, axis=-1, keepdims=True)
    m = jnp.maximum(jnp.max(s, axis=-1, keepdims=True), s_new)
    p = jnp.exp2(s - m)
    p_new = jnp.exp2(s_new - m)
    inv_l = 1.0 / (jnp.sum(p, axis=-1, keepdims=True) + p_new)
    pb = p.astype(bf16)
    heads = []
    for h in range(DIFF_HEADS):
        vh = jnp.concatenate([vbuf[slot, j, pl.ds(h, page, stride=DIFF_HEADS), :] for j in range(n_pages)], axis=0)
        oh = jnp.dot(pb, vh.astype(bf16), preferred_element_type=f32)
        oh = (oh + p_new * v_new[:, h * DIFF_DV:(h + 1) * DIFF_DV]) * inv_l
        heads.append(oh[2 * h:2 * h + 1, :] - lam * oh[2 * h + 1:2 * h + 2, :])
    return jnp.concatenate(heads, axis=-1)


def _page_copies(pt_ref, ck_hbm, cv_hbm, kbuf, vbuf, sem, sample, slot, *, layer, n_pages):
    out = []
    for j in range(n_pages):
        phys = pt_ref[sample * n_pages + j]
        out.append(pltpu.make_async_copy(ck_hbm.at[layer, phys], kbuf.at[slot, j], sem.at[0, slot]))
        out.append(pltpu.make_async_copy(cv_hbm.at[layer, phys], vbuf.at[slot, j], sem.at[1, slot]))
    return out


def _post_attn_kernel(pt_ref, x_ref, od_ref, og_ref, gg_ref, sub_ref, gn_ref, wo_ref, n2_ref, wgu_ref, wd_ref,
                      q_ref, kn_ref, vn_ref, lqk_ref, ck_hbm, cv_hbm, y_ref, os_ref, kbuf, vbuf, sem,
                      *, lam_init, d_ff, ff_chunk, layer, n_pages, page, per_step):
    step = pl.program_id(0)
    nsteps = pl.num_programs(0)
    copies = functools.partial(_page_copies, pt_ref, ck_hbm, cv_hbm, kbuf, vbuf, sem, layer=layer, n_pages=n_pages)

    @pl.when(step == 0)
    def _():
        for cp in copies(0, 0):
            cp.start()

    lam = _diff_lambda(lqk_ref[...], lam_init)

    def sample(j):
        g = step * per_step + j
        slot = j % 2
        if j + 1 < per_step:
            for cp in copies(g + 1, 1 - slot):
                cp.start()
        else:
            @pl.when(step + 1 < nsteps)
            def _():
                for cp in copies(g + 1, 1 - slot):
                    cp.start()
        for cp in copies(g, slot):
            cp.wait()
        os_ref[j] = _sample_attention(q_ref[j], kn_ref[j], vn_ref[j], lam, kbuf, vbuf, slot,
                                      n_pages=n_pages, page=page)

    acc, n2 = _merge_ffn_pieces(x_ref, od_ref, og_ref, gg_ref, sub_ref, gn_ref, wo_ref, n2_ref, lam_init)
    nchunk = d_ff // ff_chunk
    sample_at = {(j * nchunk) // per_step: j for j in range(per_step)}
    for ic in range(nchunk):
        if ic in sample_at:
            sample(sample_at[ic])
        acc = acc + _ffn_chunk(n2, wgu_ref, wd_ref, ic * ff_chunk, ff_chunk, d_ff)
    y_ref[...] = acc


def _post_attn(x2d, od, og, gg, sub, gn, w_out, n2, w_gu, w_down, page_table, q_s, kn_s, vn_s, lqk, cache_kt, cache_vr,
               *, tm, lam_init, layer):
    n, d = x2d.shape
    d_ff = w_down.shape[1]
    nsteps = n // tm
    nb, n_pages = page_table.shape
    per_step = nb // nsteps
    assert per_step * nsteps == nb and per_step % 2 == 0
    page = cache_kt.shape[-1]
    nrow = 2 * DIFF_HEADS
    row = lambda w: pl.BlockSpec((tm, w), lambda i, pt: (i, 0))
    const = lambda shape: pl.BlockSpec(shape, lambda i, pt: (0,) * len(shape), pipeline_mode=pl.Buffered(1))
    per = lambda r, w: pl.BlockSpec((per_step, r, w), lambda i, pt: (i, 0, 0))
    grid_spec = pltpu.PrefetchScalarGridSpec(
        num_scalar_prefetch=1,
        grid=(nsteps,),
        in_specs=[row(d), row(DV_W), row(GV_W), row(GV_W), const((1, DV_W)), const((1, GV_W)),
                  _layer_spec(w_out.shape, layer), const((1, d)), _layer_spec(w_gu.shape, layer),
                  _layer_spec(w_down.shape, layer),
                  per(nrow, DIFF_DK), per(nrow, DIFF_DK), per(1, DV_W), const(lqk.shape),
                  pl.BlockSpec(memory_space=pl.ANY), pl.BlockSpec(memory_space=pl.ANY)],
        out_specs=(row(d), per(1, DV_W)),
        scratch_shapes=[pltpu.VMEM((2, n_pages) + cache_kt.shape[2:], f32),
                        pltpu.VMEM((2, n_pages) + cache_vr.shape[2:], f32),
                        pltpu.SemaphoreType.DMA((2, 2))],
    )
    y, o_s = pl.pallas_call(
        functools.partial(_post_attn_kernel, lam_init=lam_init, d_ff=d_ff, ff_chunk=2 * LANES, layer=layer,
                          n_pages=n_pages, page=page, per_step=per_step),
        grid_spec=grid_spec,
        out_shape=(jax.ShapeDtypeStruct((n, d), f32), jax.ShapeDtypeStruct((nb, 1, DV_W), f32)),
        compiler_params=pltpu.CompilerParams(dimension_semantics=("arbitrary",), vmem_limit_bytes=FUSED_VMEM_LIMIT),
        name="post_attn",
    )(page_table.reshape(-1), x2d, od, og, gg, sub, gn, w_out, n2, w_gu, w_down,
      q_s.reshape(nb, nrow, DIFF_DK), kn_s.reshape(nb, nrow, DIFF_DK), vn_s.reshape(nb, 1, DV_W), lqk,
      cache_kt, cache_vr)
    return y, o_s.reshape(nb, DV_W)


def _gla_sample_kernel(qt_ref, kt_ref, v_ref, et_ref, s0_ref, o_ref, s_ref, *, tb):
    qt = qt_ref[0]
    kt = kt_ref[0]
    et = et_ref[0]
    v = v_ref[...]
    for i in range(tb):
        for h in range(GLA_HEADS):
            ks = slice(h * GLA_DK, (h + 1) * GLA_DK)
            vs = slice(h * GLA_DV, (h + 1) * GLA_DV)
            s_new = s0_ref[0, i, h] * et[ks, i:i + 1] + kt[ks, i:i + 1] * v[i:i + 1, vs]
            s_ref[i, h] = s_new
            o_ref[i:i + 1, vs] = jnp.sum(qt[ks, i:i + 1] * s_new, axis=0, keepdims=True)


def _gla_sample(gqt, gkt, gv, et, s0_all, *, layer):
    nblk, _, tb = gqt.shape
    tr = pl.BlockSpec((1, GK_W, tb), lambda i: (i, 0, 0))
    row = pl.BlockSpec((tb, GV_W), lambda i: (i, 0))
    st_in = pl.BlockSpec((1, tb, GLA_HEADS, GLA_DK, GLA_DV), lambda i: (layer, i, 0, 0, 0))
    st = pl.BlockSpec((tb, GLA_HEADS, GLA_DK, GLA_DV), lambda i: (i, 0, 0, 0))
    return pl.pallas_call(
        functools.partial(_gla_sample_kernel, tb=tb),
        grid=(nblk,),
        in_specs=[tr, tr, row, tr, st_in],
        out_specs=(row, st),
        out_shape=(jax.ShapeDtypeStruct((nblk * tb, GV_W), f32), jax.ShapeDtypeStruct(s0_all.shape[1:], f32)),
        compiler_params=_cparams(("parallel",)),
        name="gla_sample",
    )(gqt, gkt, gv, et, s0_all)


def _rope_tables(pos):
    half = DIFF_DK // 2
    freqs = ROPE_THETA ** (-jnp.arange(half, dtype=f32) / half)
    ang = pos.astype(f32)[:, None] * freqs[None, :]
    cos, sin = jnp.cos(ang), jnp.sin(ang)
    reps = LANES // DIFF_DK
    return jnp.tile(jnp.concatenate([cos, cos], -1), (1, reps)), jnp.tile(jnp.concatenate([-sin, sin], -1), (1, reps))


def kernel(x_prompt, x_sample, cache_k, cache_v, state_gla, page_table, norm1, w_in, q_norm, k_norm, lambda_qk,
           subln, w_a2, b_a, gla_norm, w_out, norm2, w_gu, w_down):
    batch, seq, d = x_prompt.shape
    nb, dec_seq, _ = x_sample.shape
    assert dec_seq == 1
    depth = w_in.shape[0]
    n_pages = page_table.shape[1]
    page = cache_k.shape[2]
    past = n_pages * page
    rank = w_a2.shape[1]

    cos_p, sin_p = _rope_tables(jnp.arange(seq))
    cos_s, sin_s = _rope_tables(past + jnp.arange(dec_seq))
    cos_s = jnp.broadcast_to(cos_s, (nb, LANES))
    sin_s = jnp.broadcast_to(sin_s, (nb, LANES))

    ck = jnp.transpose(cache_k, (0, 1, 3, 4, 5, 2))
    cv = cache_v.reshape(depth, -1, page * DIFF_HEADS, DIFF_DV)

    yp = x_prompt.reshape(batch * seq, d)
    ys = x_sample.reshape(nb, d)
    w_in_b = w_in.astype(bf16)
    w_r = jnp.pad(w_in[:, :, MAIN_W:], ((0, 0), (0, 0), (0, LANES - rank))).astype(bf16)
    wa2 = jnp.pad(w_a2, ((0, 0), (0, LANES - rank), (0, 0))).astype(bf16)
    w_out_b, w_gu_b, w_down_b = w_out.astype(bf16), w_gu.astype(bf16), w_down.astype(bf16)

    kp, vp, sp, k_s, v_s, s_s = [], [], [], [], [], []
    for l in range(depth):
        lam_init = 0.8 - 0.6 * math.exp(-0.3 * l)
        proj_args = (norm1[l][None], w_in_b, w_r, wa2, b_a[l][None],
                     jnp.tile(q_norm[l], QK_W // DIFF_DK)[None], jnp.tile(k_norm[l], QK_W // DIFF_DK)[None])
        post_args = (jnp.tile(subln[l], DIFF_HEADS)[None], jnp.tile(gla_norm[l], GLA_HEADS)[None],
                     w_out_b, norm2[l][None], w_gu_b, w_down_b)

        q, kb, kt, vb, v4, gq, gk, gv, gl, gg = _proj_prompt(yp, *proj_args, cos_p, sin_p, layer=l, batch=batch,
                                                             tm=512)
        q_s, k_s_l, v_s_l, gqt, gkt, gv_s, et, gg_s = _proj_sample(ys, *proj_args, cos_s, sin_s, layer=l,
                                                                   tb=GLA_SAMPLE_BLOCK)
        od = _attn_prompt(q, kb, vb, lambda_qk[l], batch=batch, seq=seq, tq=512, lam_init=lam_init)
        og, s_fin = _gla_prompt(gq, gk, gv, gl, batch=batch, seq=seq, tg=512)
        yp, od_s = _post_attn(yp, od, og, gg, *post_args, page_table, q_s, k_s_l, v_s_l, lambda_qk[l], ck, cv,
                              tm=512, lam_init=lam_init, layer=l)
        kp.append(kt)
        vp.append(v4)
        sp.append(s_fin)

        og_s, s_new = _gla_sample(gqt, gkt, gv_s, et, state_gla, layer=l)
        ys = _post(ys, od_s, og_s, gg_s, *post_args, layer=l, tm=nb, lam_init=lam_init)
        k_s.append(k_s_l.reshape(nb, dec_seq, DIFF_HEADS, 2, DIFF_DK))
        v_s.append(v_s_l.reshape(nb, dec_seq, DIFF_HEADS, DIFF_DV))
        s_s.append(s_new)

    k_prompt = jnp.stack(kp).reshape(depth, batch, DIFF_HEADS, 2, DIFF_DK, seq).transpose(0, 1, 5, 2, 3, 4)
    v_prompt = jnp.stack(vp).reshape(depth, batch, seq, DIFF_HEADS, DIFF_DV)
    return (yp.reshape(batch, seq, d), ys.reshape(nb, dec_seq, d), k_prompt, v_prompt, jnp.stack(sp),
            jnp.stack(k_s), jnp.stack(v_s), jnp.stack(s_s))
```

```python
import functools
import math

import jax
import jax.numpy as jnp
from jax import lax
from jax.experimental import pallas as pl
from jax.experimental.pallas import tpu as pltpu

f32 = jnp.float32
bf16 = jnp.bfloat16

DIFF_HEADS = 4
DIFF_DK = 64
DIFF_DV = 128
GLA_HEADS = 4
GLA_DK = 64
GLA_DV = 128
GLA_GATE_NORM = 16.0
ROPE_THETA = 10000.0
EPS = 1e-6
LOG2_E = math.log2(math.e)

QK_W = DIFF_HEADS * 2 * DIFF_DK
DV_W = DIFF_HEADS * DIFF_DV
GK_W = GLA_HEADS * GLA_DK
GV_W = GLA_HEADS * GLA_DV
MAIN_W = 2 * QK_W + DV_W + 2 * GK_W + 2 * GV_W

LANES = 128
GLA_SAMPLE_BLOCK = 32
GLA_BLOCK = 256
GLA_SAFE_LOG_DECAY = 80.0
VMEM_LIMIT = 56 * 1024 * 1024
FUSED_VMEM_LIMIT = 60 * 1024 * 1024


def _cparams(sem):
    return pltpu.CompilerParams(dimension_semantics=sem, vmem_limit_bytes=VMEM_LIMIT)


def _const_spec(shape):
    nd = len(shape)
    return pl.BlockSpec(shape, lambda *_: (0,) * nd, pipeline_mode=pl.Buffered(1))


def _layer_spec(shape, layer):
    nd = len(shape)
    return pl.BlockSpec((1,) + tuple(shape[1:]), lambda *_: (layer,) + (0,) * (nd - 1), pipeline_mode=pl.Buffered(1))


def _group_scale(z, width):
    lane = lax.broadcasted_iota(jnp.int32, (z.shape[0], LANES), 1)
    cols = []
    for c in range(z.shape[1] // LANES):
        zc = z[:, c * LANES:(c + 1) * LANES]
        zz = zc * zc
        if width == LANES:
            cols.append(jnp.broadcast_to(lax.rsqrt(jnp.mean(zz, axis=-1, keepdims=True) + EPS), zc.shape))
        else:
            lo = lane < width
            s_lo = jnp.sum(jnp.where(lo, zz, 0.0), axis=-1, keepdims=True)
            s_hi = jnp.sum(jnp.where(lo, 0.0, zz), axis=-1, keepdims=True)
            r_lo = lax.rsqrt(s_lo * (1.0 / width) + EPS)
            r_hi = lax.rsqrt(s_hi * (1.0 / width) + EPS)
            cols.append(jnp.where(lo, r_lo, r_hi))
    return jnp.concatenate(cols, axis=-1)


def _rope(z, cos, sin_signed):
    lane = lax.broadcasted_iota(jnp.int32, (z.shape[0], LANES), 1)
    first_half = (lane % DIFF_DK) < (DIFF_DK // 2)
    cols = []
    for c in range(z.shape[1] // LANES):
        zc = z[:, c * LANES:(c + 1) * LANES]
        partner = jnp.where(first_half, pltpu.roll(zc, LANES - DIFF_DK // 2, 1), pltpu.roll(zc, DIFF_DK // 2, 1))
        cols.append(zc * cos + partner * sin_signed)
    return jnp.concatenate(cols, axis=-1)


def _diff_lambda(lqk, lam_init):
    a = jnp.sum(lqk[0:1, :] * lqk[1:2, :], axis=-1, keepdims=True)
    b = jnp.sum(lqk[2:3, :] * lqk[3:4, :], axis=-1, keepdims=True)
    return jnp.exp(a) - jnp.exp(b) + lam_init


def _proj_body(x, g1_ref, w_ref, wr_ref, wa2_ref, ba_ref, qn_ref, kn_ref, cos, sin):
    n = x * lax.rsqrt(jnp.mean(x * x, axis=-1, keepdims=True) + EPS) * g1_ref[...]
    nb = n.astype(bf16)

    def seg(lo, width):
        return jnp.dot(nb, w_ref[0, :, lo:lo + width], preferred_element_type=f32)

    zq = seg(0, QK_W)
    q = _rope(zq * _group_scale(zq, DIFF_DK) * qn_ref[...], cos, sin) * (DIFF_DK ** -0.5 * LOG2_E)
    zk = seg(QK_W, QK_W)
    k = _rope(zk * _group_scale(zk, DIFF_DK) * kn_ref[...], cos, sin)
    v = seg(2 * QK_W, DV_W)
    off = 2 * QK_W + DV_W
    gq = seg(off, GK_W) * (GLA_DK ** -0.5)
    gk = seg(off + GK_W, GK_W)
    gv = seg(off + 2 * GK_W, GV_W)
    gg = seg(off + 2 * GK_W + GV_W, GV_W)
    r = jnp.dot(nb, wr_ref[0], preferred_element_type=f32)
    a = jnp.dot(r.astype(bf16), wa2_ref[0], preferred_element_type=f32) + ba_ref[...]
    gl = (jnp.minimum(a, 0.0) - jnp.log1p(jnp.exp(-jnp.abs(a)))) * (1.0 / GLA_GATE_NORM)
    return q, k, v, gq, gk, gv, gl, gg


def _proj_prompt_kernel(x_ref, g1_ref, w_ref, wr_ref, wa2_ref, ba_ref, qn_ref, kn_ref, cos_ref, sin_ref,
                        q_ref, kb_ref, kt_ref, vb_ref, v4_ref, gq_ref, gk_ref, gv_ref, gl_ref, gg_ref):
    tm = x_ref.shape[0]
    q, k, v, gq, gk, gv, gl, gg = _proj_body(x_ref[...], g1_ref, w_ref, wr_ref, wa2_ref, ba_ref, qn_ref, kn_ref,
                                             cos_ref[...], sin_ref[...])
    first_map = lax.broadcasted_iota(jnp.int32, (tm, LANES), 1) < DIFF_DK
    for h in range(DIFF_HEADS):
        qh = q[:, h * LANES:(h + 1) * LANES]
        q_ref[:, (2 * h) * LANES:(2 * h + 1) * LANES] = jnp.where(first_map, qh, 0.0).astype(bf16)
        q_ref[:, (2 * h + 1) * LANES:(2 * h + 2) * LANES] = jnp.where(first_map, 0.0, qh).astype(bf16)
    kb_ref[...] = k.astype(bf16)
    kt_ref[0] = k.T
    vb_ref[...] = v.astype(bf16)
    for h in range(DIFF_HEADS):
        v4_ref[pl.ds(h, tm, stride=DIFF_HEADS), :] = v[:, h * DIFF_DV:(h + 1) * DIFF_DV]
    gq_ref[...] = gq
    gk_ref[...] = gk
    gv_ref[...] = gv.astype(bf16)
    gl_ref[...] = gl
    gg_ref[...] = gg


def _proj_prompt(x2d, g1, w_main, w_r, w_a2, b_a, qn, kn, cos_t, sin_t, *, layer, batch, tm):
    n, d = x2d.shape
    seq = n // batch
    nt = seq // tm
    row = lambda w: pl.BlockSpec((tm, w), lambda b, j: (b * nt + j, 0))
    tab = pl.BlockSpec((tm, LANES), lambda b, j: (j, 0))
    arr = lambda w, dt: jax.ShapeDtypeStruct((n, w), dt)
    out_shape = (
        arr(2 * QK_W, bf16),
        arr(QK_W, bf16),
        jax.ShapeDtypeStruct((batch, QK_W, seq), f32),
        arr(DV_W, bf16),
        jax.ShapeDtypeStruct((n * DIFF_HEADS, DIFF_DV), f32),
        arr(GK_W, f32), arr(GK_W, f32), arr(GV_W, bf16), arr(GK_W, f32), arr(GV_W, f32),
    )
    out_specs = (row(2 * QK_W), row(QK_W), pl.BlockSpec((1, QK_W, tm), lambda b, j: (b, 0, j)), row(DV_W),
                 pl.BlockSpec((tm * DIFF_HEADS, DIFF_DV), lambda b, j: (b * nt + j, 0)),
                 row(GK_W), row(GK_W), row(GV_W), row(GK_W), row(GV_W))
    return pl.pallas_call(
        _proj_prompt_kernel,
        grid=(batch, nt),
        in_specs=[row(d), _const_spec((1, d)), _layer_spec(w_main.shape, layer), _layer_spec(w_r.shape, layer),
                  _layer_spec(w_a2.shape, layer), _const_spec((1, GK_W)), _const_spec((1, QK_W)),
                  _const_spec((1, QK_W)), tab, tab],
        out_specs=out_specs,
        out_shape=out_shape,
        compiler_params=_cparams(("parallel", "parallel")),
        name="proj_prompt",
    )(x2d, g1, w_main, w_r, w_a2, b_a, qn, kn, cos_t, sin_t)


def _proj_sample_kernel(x_ref, g1_ref, w_ref, wr_ref, wa2_ref, ba_ref, qn_ref, kn_ref, cos_ref, sin_ref,
                        q_ref, k_ref, v_ref, gqt_ref, gkt_ref, gv_ref, et_ref, gg_ref):
    q, k, v, gq, gk, gv, gl, gg = _proj_body(x_ref[...], g1_ref, w_ref, wr_ref, wa2_ref, ba_ref, qn_ref, kn_ref,
                                             cos_ref[...], sin_ref[...])
    q_ref[...] = q
    k_ref[...] = k
    v_ref[...] = v
    gv_ref[...] = gv
    gg_ref[...] = gg
    e = jnp.exp(gl)
    tb = gqt_ref.shape[2]
    for blk in range(gqt_ref.shape[0]):
        rows = slice(blk * tb, (blk + 1) * tb)
        gqt_ref[blk] = gq[rows, :].T
        gkt_ref[blk] = gk[rows, :].T
        et_ref[blk] = e[rows, :].T


def _proj_sample(x2d, g1, w_main, w_r, w_a2, b_a, qn, kn, cos_t, sin_t, *, layer, tb):
    n, d = x2d.shape
    flat = lambda w: jax.ShapeDtypeStruct((n, w), f32)
    tr = jax.ShapeDtypeStruct((n // tb, GK_W, tb), f32)
    out_shape = (flat(QK_W), flat(QK_W), flat(DV_W), tr, tr, flat(GV_W), tr, flat(GV_W))
    whole = lambda shape: pl.BlockSpec(shape, lambda i: (0,) * len(shape))
    return pl.pallas_call(
        _proj_sample_kernel,
        grid=(1,),
        in_specs=[whole(x2d.shape), whole((1, d)), _layer_spec(w_main.shape, layer), _layer_spec(w_r.shape, layer),
                  _layer_spec(w_a2.shape, layer), whole((1, GK_W)), whole((1, QK_W)), whole((1, QK_W)),
                  whole(cos_t.shape), whole(sin_t.shape)],
        out_specs=tuple(whole(o.shape) for o in out_shape),
        out_shape=out_shape,
        compiler_params=_cparams(("arbitrary",)),
        name="proj_sample",
    )(x2d, g1, w_main, w_r, w_a2, b_a, qn, kn, cos_t, sin_t)


def _attn_kernel(q_ref, kb_ref, v_ref, lqk_ref, o_ref, vb_ref, s_ref, *, tq, heads, lam_init):
    seq = kb_ref.shape[0]
    lam = _diff_lambda(lqk_ref[...], lam_init)
    row = lax.broadcasted_iota(jnp.int32, (2 * tq, tq), 0) % tq
    col = lax.broadcasted_iota(jnp.int32, (2 * tq, tq), 1)
    visible = col <= row
    nq = seq // tq
    for hh in range(heads):
        kcols = slice(hh * LANES, (hh + 1) * LANES)
        vb_ref[hh, :, :DIFF_DV] = v_ref[:, kcols]
        vb_ref[hh, :, DIFF_DV:] = jnp.ones((seq, LANES), bf16)
        for qi in (range(nq) if hh % 2 == 0 else reversed(range(nq))):
            rows = slice(qi * tq, (qi + 1) * tq)
            qs = jnp.concatenate([q_ref[rows, (2 * hh) * LANES:(2 * hh + 1) * LANES],
                                  q_ref[rows, (2 * hh + 1) * LANES:(2 * hh + 2) * LANES]], axis=0)
            mx = None
            for kt in range(qi + 1):
                cols = slice(kt * tq, (kt + 1) * tq)
                s = lax.dot_general(qs, kb_ref[cols, kcols], (((1,), (1,)), ((), ())), preferred_element_type=f32)
                if kt == qi:
                    s = jnp.where(visible, s, -jnp.inf)
                s_ref[hh, :, cols] = s
                for c in range(tq // LANES):
                    sc = s[:, c * LANES:(c + 1) * LANES]
                    mx = sc if mx is None else jnp.maximum(mx, sc)
            m = jnp.broadcast_to(jnp.max(mx, axis=-1, keepdims=True), (2 * tq, tq))
            acc = None
            for kt in range(qi + 1):
                cols = slice(kt * tq, (kt + 1) * tq)
                p = jnp.exp2(s_ref[hh, :, cols] - m).astype(bf16)
                pv = jnp.dot(p, vb_ref[hh, cols, :], preferred_element_type=f32)
                acc = pv if acc is None else acc + pv
            o1 = acc[:tq, :DIFF_DV] / acc[:tq, DIFF_DV:]
            o2 = acc[tq:, :DIFF_DV] / acc[tq:, DIFF_DV:]
            o_ref[rows, hh * DIFF_DV:(hh + 1) * DIFF_DV] = o1 - lam * o2


def _attn_prompt(q, kb, vb, lqk, *, batch, seq, tq, heads, lam_init):
    return pl.pallas_call(
        functools.partial(_attn_kernel, tq=tq, heads=heads, lam_init=lam_init),
        grid=(batch, DIFF_HEADS // heads),
        in_specs=[pl.BlockSpec((seq, heads * 2 * LANES), lambda b, g: (b, g)),
                  pl.BlockSpec((seq, heads * LANES), lambda b, g: (b, g)),
                  pl.BlockSpec((seq, heads * LANES), lambda b, g: (b, g)),
                  _const_spec(lqk.shape)],
        out_specs=pl.BlockSpec((seq, heads * DIFF_DV), lambda b, g: (b, g)),
        out_shape=jax.ShapeDtypeStruct((batch * seq, DV_W), f32),
        scratch_shapes=[pltpu.VMEM((heads, seq, 2 * LANES), bf16), pltpu.VMEM((heads, 2 * tq, seq), f32)],
        compiler_params=_cparams(("parallel", "parallel")),
        name="attn_prompt",
    )(q, kb, vb, lqk)


def _cumsum_rows(g, tril):
    g1 = g.astype(bf16)
    r1 = g - g1.astype(f32)
    g2 = r1.astype(bf16)
    g3 = (r1 - g2.astype(f32)).astype(bf16)
    dot = lambda t: jnp.dot(tril, t, preferred_element_type=f32)
    return dot(g1) + dot(g2) + dot(g3)


def _gla_kernel(q_ref, k_ref, v_ref, g_ref, o_ref, s_ref, st_ref, b_ref, *, tg):
    step = pl.program_id(1)

    @pl.when(step == 0)
    def _():
        st_ref[...] = jnp.zeros(st_ref.shape, f32)

    c = GLA_BLOCK
    nblock = tg // c
    ri = lax.broadcasted_iota(jnp.int32, (c, c), 0)
    ci = lax.broadcasted_iota(jnp.int32, (c, c), 1)
    causal = ri >= ci
    tril = jnp.where(causal, 1.0, 0.0).astype(bf16)
    head0 = lax.broadcasted_iota(jnp.int32, (c, LANES), 1) < GLA_DK
    lane_sq = lax.broadcasted_iota(jnp.int32, (LANES, LANES), 1) < GLA_DK
    block_rows = lambda ib: slice(ib * c, (ib + 1) * c)
    head_cols = lambda h: slice(h * GLA_DV, (h + 1) * GLA_DV)
    head_mask = lambda j: head0 if j == 0 else jnp.logical_not(head0)

    worst = None
    for ib in range(nblock):
        b = _cumsum_rows(g_ref[block_rows(ib), :], tril)
        b_ref[block_rows(ib), :] = b
        b_mid = b[c // 2 - 1:c // 2, :]
        w = jnp.maximum(-b_mid, b_mid - b[c - 1:c, :])
        worst = w if worst is None else jnp.maximum(worst, w)
    safe = jnp.max(worst) < GLA_SAFE_LOG_DECAY

    @pl.when(safe)
    def _():
        for ib in range(nblock):
            rows = block_rows(ib)
            b = b_ref[rows, :]
            b_mid = b[c // 2 - 1:c // 2, :]
            qe = q_ref[rows, :] * jnp.exp(b - b_mid)
            ke = (k_ref[rows, :] * jnp.exp(b_mid - b)).astype(bf16)
            for h in range(GLA_HEADS):
                sl = slice((h // 2) * LANES, (h // 2 + 1) * LANES)
                qe_h = jnp.where(head_mask(h % 2), qe[:, sl], 0.0).astype(bf16)
                a = lax.dot_general(qe_h, ke[:, sl], (((1,), (1,)), ((), ())), preferred_element_type=f32)
                a = jnp.where(causal, a, 0.0).astype(bf16)
                o_ref[rows, head_cols(h)] = jnp.dot(a, v_ref[rows, head_cols(h)], preferred_element_type=f32)

    @pl.when(jnp.logical_not(safe))
    def _():
        tok = lax.broadcasted_iota(jnp.int32, (c, 1), 0)

        def exact_block(ib, carry):
            rows = pl.ds(pl.multiple_of(ib * c, c), c)
            b = b_ref[rows, :]
            q = q_ref[rows, :]
            k = k_ref[rows, :]
            v = v_ref[rows, :].astype(f32)
            pick = lambda x, s: jnp.sum(jnp.where(tok == s, x, 0.0), axis=0, keepdims=True)

            def key(s, acc):
                w = q * pick(k, s) * jnp.exp(jnp.minimum(b - pick(b, s), 0.0))
                w = jnp.where(tok >= s, w, 0.0)
                vs = pick(v, s)
                return tuple(
                    acc[h] + jnp.sum(w[:, h * GLA_DK:(h + 1) * GLA_DK], axis=-1, keepdims=True) * vs[:, head_cols(h)]
                    for h in range(GLA_HEADS))

            acc = lax.fori_loop(0, c, key, tuple(jnp.zeros((c, GLA_DV), f32) for _ in range(GLA_HEADS)))
            for h in range(GLA_HEADS):
                o_ref[rows, head_cols(h)] = acc[h]
            return carry

        lax.fori_loop(0, nblock, exact_block, 0)

    for ib in range(nblock):
        rows = block_rows(ib)
        b = b_ref[rows, :]
        b_last = b[c - 1:c, :]
        qb = q_ref[rows, :] * jnp.exp(b)
        kd = (k_ref[rows, :] * jnp.exp(b_last - b)).astype(bf16)
        e_last = jnp.exp(b_last)
        for p in range(GLA_HEADS // 2):
            sl = slice(p * LANES, (p + 1) * LANES)
            st = st_ref[p]
            stb = st.astype(bf16)
            upd = []
            for j in range(2):
                h = 2 * p + j
                qb_h = jnp.where(head_mask(j), qb[:, sl], 0.0).astype(bf16)
                o_ref[rows, head_cols(h)] += lax.dot_general(qb_h, stb, (((1,), (1,)), ((), ())),
                                                             preferred_element_type=f32)
                upd.append(lax.dot_general(v_ref[rows, head_cols(h)], kd[:, sl], (((0,), (0,)), ((), ())),
                                           preferred_element_type=f32))
            st_ref[p] = st * e_last[:, sl] + jnp.where(lane_sq, upd[0], upd[1])

    @pl.when(step == pl.num_programs(1) - 1)
    def _():
        for p in range(GLA_HEADS // 2):
            t = st_ref[p].T
            s_ref[0, 2 * p] = t[:GLA_DK, :]
            s_ref[0, 2 * p + 1] = t[GLA_DK:, :]


def _gla_prompt(gq, gk, gv, gl, *, batch, seq, tg):
    ns = seq // tg
    row = lambda w: pl.BlockSpec((tg, w), lambda b, i: (b * ns + i, 0))
    return pl.pallas_call(
        functools.partial(_gla_kernel, tg=tg),
        grid=(batch, ns),
        in_specs=[row(GK_W), row(GK_W), row(GV_W), row(GK_W)],
        out_specs=(row(GV_W), pl.BlockSpec((1, GLA_HEADS, GLA_DK, GLA_DV), lambda b, i: (b, 0, 0, 0))),
        out_shape=(jax.ShapeDtypeStruct((batch * seq, GV_W), f32),
                   jax.ShapeDtypeStruct((batch, GLA_HEADS, GLA_DK, GLA_DV), f32)),
        scratch_shapes=[pltpu.VMEM((GLA_HEADS // 2, LANES, LANES), f32), pltpu.VMEM((tg, GK_W), f32)],
        compiler_params=_cparams(("parallel", "arbitrary")),
        name="gla_prompt",
    )(gq, gk, gv, gl)


def _merge_ffn_pieces(x_ref, od_ref, og_ref, gg_ref, sub_ref, gn_ref, wo_ref, n2_ref, lam_init):
    od = od_ref[...]
    odn = od * _group_scale(od, DIFF_DV) * sub_ref[...] * (1.0 - lam_init)
    og = og_ref[...]
    ogn = og * _group_scale(og, GLA_DV) * gn_ref[...]
    gg = gg_ref[...]
    ogn = ogn * (gg * jax.nn.sigmoid(gg))
    mix = jnp.concatenate([odn, ogn], axis=-1).astype(bf16)
    y = x_ref[...] + jnp.dot(mix, wo_ref[0], preferred_element_type=f32)
    n2 = (y * lax.rsqrt(jnp.mean(y * y, axis=-1, keepdims=True) + EPS) * n2_ref[...]).astype(bf16)
    return y, n2


def _ffn_chunk(n2, wgu_ref, wd_ref, c0, ff_chunk, d_ff):
    a = jnp.dot(n2, wgu_ref[0, :, c0:c0 + ff_chunk], preferred_element_type=f32)
    b = jnp.dot(n2, wgu_ref[0, :, d_ff + c0:d_ff + c0 + ff_chunk], preferred_element_type=f32)
    hid = (a * jax.nn.sigmoid(a) * b).astype(bf16)
    return jnp.dot(hid, wd_ref[0, c0:c0 + ff_chunk, :], preferred_element_type=f32)


def _post_kernel(x_ref, od_ref, og_ref, gg_ref, sub_ref, gn_ref, wo_ref, n2_ref, wgu_ref, wd_ref, y_ref,
                 *, lam_init, d_ff, ff_chunk):
    acc, n2 = _merge_ffn_pieces(x_ref, od_ref, og_ref, gg_ref, sub_ref, gn_ref, wo_ref, n2_ref, lam_init)
    for c0 in range(0, d_ff, ff_chunk):
        acc = acc + _ffn_chunk(n2, wgu_ref, wd_ref, c0, ff_chunk, d_ff)
    y_ref[...] = acc


def _post(x2d, od, og, gg, sub, gn, w_out, n2, w_gu, w_down, *, layer, tm, lam_init):
    n, d = x2d.shape
    d_ff = w_down.shape[1]
    row = lambda w: pl.BlockSpec((tm, w), lambda i: (i, 0))
    return pl.pallas_call(
        functools.partial(_post_kernel, lam_init=lam_init, d_ff=d_ff, ff_chunk=2 * LANES),
        grid=(n // tm,),
        in_specs=[row(d), row(DV_W), row(GV_W), row(GV_W), _const_spec((1, DV_W)), _const_spec((1, GV_W)),
                  _layer_spec(w_out.shape, layer), _const_spec((1, d)), _layer_spec(w_gu.shape, layer),
                  _layer_spec(w_down.shape, layer)],
        out_specs=row(d),
        out_shape=jax.ShapeDtypeStruct((n, d), f32),
        compiler_params=_cparams(("parallel",)),
        name="post",
    )(x2d, od, og, gg, sub, gn, w_out, n2, w_gu, w_down)


def _sample_probs(q, k_new, kbuf, slot, *, n_pages, page):
    past = n_pages * page
    nrow = 2 * DIFF_HEADS
    qb = q.astype(bf16)
    rid = lax.broadcasted_iota(jnp.int32, (nrow, past), 0)
    s = jnp.zeros((nrow, past), f32)
    for h in range(DIFF_HEADS):
        for mp in range(2):
            kt = jnp.concatenate([kbuf[slot, j, h, mp] for j in range(n_pages)], axis=-1)
            sr = jnp.dot(qb, kt.astype(bf16), preferred_element_type=f32)
            s = jnp.where(rid == 2 * h + mp, sr, s)
    s_new = jnp.sum(q * k_new, axis=-1, keepdims=True)
    m = jnp.maximum(jnp.max(s, axis=-1, keepdims=True), s_new)
    p = jnp.exp2(s - m)
    p_new = jnp.exp2(s_new - m)
    inv_l = 1.0 / (jnp.sum(p, axis=-1, keepdims=True) + p_new)
    return p.astype(bf16), p_new, inv_l


def _sample_values(pb, p_new, inv_l, v_new, lam, vbuf, slot, *, n_pages, page):
    heads = []
    for h in range(DIFF_HEADS):
        vh = jnp.concatenate([vbuf[slot, j, pl.ds(h, page, stride=DIFF_HEADS), :] for j in range(n_pages)], axis=0)
        oh = jnp.dot(pb, vh.astype(bf16), preferred_element_type=f32)
        oh = (oh + p_new * v_new[:, h * DIFF_DV:(h + 1) * DIFF_DV]) * inv_l
        heads.append(oh[2 * h:2 * h + 1, :] - lam * oh[2 * h + 1:2 * h + 2, :])
    return jnp.concatenate(heads, axis=-1)


def _page_copies(pt_ref, ck_hbm, cv_hbm, kbuf, vbuf, sem, sample, slot, *, layer, n_pages):
    out = []
    for j in range(n_pages):
        phys = pt_ref[sample * n_pages + j]
        out.append(pltpu.make_async_copy(ck_hbm.at[layer, phys], kbuf.at[slot, j], sem.at[0, slot]))
        out.append(pltpu.make_async_copy(cv_hbm.at[layer, phys], vbuf.at[slot, j], sem.at[1, slot]))
    return out


def _post_attn_kernel(pt_ref, x_ref, od_ref, og_ref, gg_ref, sub_ref, gn_ref, wo_ref, n2_ref, wgu_ref, wd_ref,
                      q_ref, kn_ref, vn_ref, lqk_ref, ck_hbm, cv_hbm, y_ref, os_ref, kbuf, vbuf, sem,
                      *, lam_init, d_ff, ff_chunk, layer, n_pages, page, per_step):
    step = pl.program_id(0)
    nsteps = pl.num_programs(0)
    copies = functools.partial(_page_copies, pt_ref, ck_hbm, cv_hbm, kbuf, vbuf, sem, layer=layer, n_pages=n_pages)

    @pl.when(step == 0)
    def _():
        for cp in copies(0, 0):
            cp.start()

    lam = _diff_lambda(lqk_ref[...], lam_init)

    def sample_probs(j):
        g = step * per_step + j
        slot = j % 2
        if j + 1 < per_step:
            for cp in copies(g + 1, 1 - slot):
                cp.start()
        else:
            @pl.when(step + 1 < nsteps)
            def _():
                for cp in copies(g + 1, 1 - slot):
                    cp.start()
        for cp in copies(g, slot):
            cp.wait()
        return _sample_probs(q_ref[j], kn_ref[j], kbuf, slot, n_pages=n_pages, page=page)

    def sample_values(j, probs):
        os_ref[j] = _sample_values(*probs, vn_ref[j], lam, vbuf, j % 2, n_pages=n_pages, page=page)

    acc, n2 = _merge_ffn_pieces(x_ref, od_ref, og_ref, gg_ref, sub_ref, gn_ref, wo_ref, n2_ref, lam_init)
    nchunk = d_ff // ff_chunk
    sample_at = {(j * nchunk) // per_step: j for j in range(per_step)}
    for ic in range(nchunk):
        if ic in sample_at:
            probs = sample_probs(sample_at[ic])
        acc = acc + _ffn_chunk(n2, wgu_ref, wd_ref, ic * ff_chunk, ff_chunk, d_ff)
        if ic in sample_at:
            sample_values(sample_at[ic], probs)
    y_ref[...] = acc


def _post_attn(x2d, od, og, gg, sub, gn, w_out, n2, w_gu, w_down, page_table, q_s, kn_s, vn_s, lqk, cache_kt, cache_vr,
               *, tm, lam_init, layer):
    n, d = x2d.shape
    d_ff = w_down.shape[1]
    nsteps = n // tm
    nb, n_pages = page_table.shape
    per_step = nb // nsteps
    assert per_step * nsteps == nb and per_step % 2 == 0
    page = cache_kt.shape[-1]
    nrow = 2 * DIFF_HEADS
    row = lambda w: pl.BlockSpec((tm, w), lambda i, pt: (i, 0))
    const = lambda shape: pl.BlockSpec(shape, lambda i, pt: (0,) * len(shape), pipeline_mode=pl.Buffered(1))
    per = lambda r, w: pl.BlockSpec((per_step, r, w), lambda i, pt: (i, 0, 0))
    grid_spec = pltpu.PrefetchScalarGridSpec(
        num_scalar_prefetch=1,
        grid=(nsteps,),
        in_specs=[row(d), row(DV_W), row(GV_W), row(GV_W), const((1, DV_W)), const((1, GV_W)),
                  _layer_spec(w_out.shape, layer), const((1, d)), _layer_spec(w_gu.shape, layer),
                  _layer_spec(w_down.shape, layer),
                  per(nrow, DIFF_DK), per(nrow, DIFF_DK), per(1, DV_W), const(lqk.shape),
                  pl.BlockSpec(memory_space=pl.ANY), pl.BlockSpec(memory_space=pl.ANY)],
        out_specs=(row(d), per(1, DV_W)),
        scratch_shapes=[pltpu.VMEM((2, n_pages) + cache_kt.shape[2:], f32),
                        pltpu.VMEM((2, n_pages) + cache_vr.shape[2:], f32),
                        pltpu.SemaphoreType.DMA((2, 2))],
    )
    y, o_s = pl.pallas_call(
        functools.partial(_post_attn_kernel, lam_init=lam_init, d_ff=d_ff, ff_chunk=2 * LANES, layer=layer,
                          n_pages=n_pages, page=page, per_step=per_step),
        grid_spec=grid_spec,
        out_shape=(jax.ShapeDtypeStruct((n, d), f32), jax.ShapeDtypeStruct((nb, 1, DV_W), f32)),
        compiler_params=pltpu.CompilerParams(dimension_semantics=("arbitrary",), vmem_limit_bytes=FUSED_VMEM_LIMIT),
        name="post_attn",
    )(page_table.reshape(-1), x2d, od, og, gg, sub, gn, w_out, n2, w_gu, w_down,
      q_s.reshape(nb, nrow, DIFF_DK), kn_s.reshape(nb, nrow, DIFF_DK), vn_s.reshape(nb, 1, DV_W), lqk,
      cache_kt, cache_vr)
    return y, o_s.reshape(nb, DV_W)


def _gla_sample_kernel(qt_ref, kt_ref, v_ref, et_ref, s0_ref, o_ref, s_ref, *, tb):
    qt = qt_ref[0]
    kt = kt_ref[0]
    et = et_ref[0]
    v = v_ref[...]
    for i in range(tb):
        for h in range(GLA_HEADS):
            ks = slice(h * GLA_DK, (h + 1) * GLA_DK)
            vs = slice(h * GLA_DV, (h + 1) * GLA_DV)
            s_new = s0_ref[0, i, h] * et[ks, i:i + 1] + kt[ks, i:i + 1] * v[i:i + 1, vs]
            s_ref[i, h] = s_new
            o_ref[i:i + 1, vs] = jnp.sum(qt[ks, i:i + 1] * s_new, axis=0, keepdims=True)


def _gla_sample(gqt, gkt, gv, et, s0_all, *, layer):
    nblk, _, tb = gqt.shape
    tr = pl.BlockSpec((1, GK_W, tb), lambda i: (i, 0, 0))
    row = pl.BlockSpec((tb, GV_W), lambda i: (i, 0))
    st_in = pl.BlockSpec((1, tb, GLA_HEADS, GLA_DK, GLA_DV), lambda i: (layer, i, 0, 0, 0))
    st = pl.BlockSpec((tb, GLA_HEADS, GLA_DK, GLA_DV), lambda i: (i, 0, 0, 0))
    return pl.pallas_call(
        functools.partial(_gla_sample_kernel, tb=tb),
        grid=(nblk,),
        in_specs=[tr, tr, row, tr, st_in],
        out_specs=(row, st),
        out_shape=(jax.ShapeDtypeStruct((nblk * tb, GV_W), f32), jax.ShapeDtypeStruct(s0_all.shape[1:], f32)),
        compiler_params=_cparams(("parallel",)),
        name="gla_sample",
    )(gqt, gkt, gv, et, s0_all)


def _rope_tables(pos):
    half = DIFF_DK // 2
    freqs = ROPE_THETA ** (-jnp.arange(half, dtype=f32) / half)
    ang = pos.astype(f32)[:, None] * freqs[None, :]
    cos, sin = jnp.cos(ang), jnp.sin(ang)
    reps = LANES // DIFF_DK
    return jnp.tile(jnp.concatenate([cos, cos], -1), (1, reps)), jnp.tile(jnp.concatenate([-sin, sin], -1), (1, reps))


def kernel(x_prompt, x_sample, cache_k, cache_v, state_gla, page_table, norm1, w_in, q_norm, k_norm, lambda_qk,
           subln, w_a2, b_a, gla_norm, w_out, norm2, w_gu, w_down):
    batch, seq, d = x_prompt.shape
    nb, dec_seq, _ = x_sample.shape
    assert dec_seq == 1
    depth = w_in.shape[0]
    n_pages = page_table.shape[1]
    page = cache_k.shape[2]
    past = n_pages * page
    rank = w_a2.shape[1]

    cos_p, sin_p = _rope_tables(jnp.arange(seq))
    cos_s, sin_s = _rope_tables(past + jnp.arange(dec_seq))
    cos_s = jnp.broadcast_to(cos_s, (nb, LANES))
    sin_s = jnp.broadcast_to(sin_s, (nb, LANES))

    ck = jnp.transpose(cache_k, (0, 1, 3, 4, 5, 2))
    cv = cache_v.reshape(depth, -1, page * DIFF_HEADS, DIFF_DV)

    yp = x_prompt.reshape(batch * seq, d)
    ys = x_sample.reshape(nb, d)
    w_in_b = w_in.astype(bf16)
    w_r = jnp.pad(w_in[:, :, MAIN_W:], ((0, 0), (0, 0), (0, LANES - rank))).astype(bf16)
    wa2 = jnp.pad(w_a2, ((0, 0), (0, LANES - rank), (0, 0))).astype(bf16)
    w_out_b, w_gu_b, w_down_b = w_out.astype(bf16), w_gu.astype(bf16), w_down.astype(bf16)

    kp, vp, sp, k_s, v_s, s_s = [], [], [], [], [], []
    for l in range(depth):
        lam_init = 0.8 - 0.6 * math.exp(-0.3 * l)
        proj_args = (norm1[l][None], w_in_b, w_r, wa2, b_a[l][None],
                     jnp.tile(q_norm[l], QK_W // DIFF_DK)[None], jnp.tile(k_norm[l], QK_W // DIFF_DK)[None])
        post_args = (jnp.tile(subln[l], DIFF_HEADS)[None], jnp.tile(gla_norm[l], GLA_HEADS)[None],
                     w_out_b, norm2[l][None], w_gu_b, w_down_b)

        q, kb, kt, vb, v4, gq, gk, gv, gl, gg = _proj_prompt(yp, *proj_args, cos_p, sin_p, layer=l, batch=batch,
                                                             tm=512)
        q_s, k_s_l, v_s_l, gqt, gkt, gv_s, et, gg_s = _proj_sample(ys, *proj_args, cos_s, sin_s, layer=l,
                                                                   tb=GLA_SAMPLE_BLOCK)
        od = _attn_prompt(q, kb, vb, lambda_qk[l], batch=batch, seq=seq, tq=256, heads=2, lam_init=lam_init)
        og, s_fin = _gla_prompt(gq, gk, gv, gl, batch=batch, seq=seq, tg=512)
        yp, od_s = _post_attn(yp, od, og, gg, *post_args, page_table, q_s, k_s_l, v_s_l, lambda_qk[l], ck, cv,
                              tm=512, lam_init=lam_init, layer=l)
        kp.append(kt)
        vp.append(v4)
        sp.append(s_fin)

        og_s, s_new = _gla_sample(gqt, gkt, gv_s, et, state_gla, layer=l)
        ys = _post(ys, od_s, og_s, gg_s, *post_args, layer=l, tm=nb, lam_init=lam_init)
        k_s.append(k_s_l.reshape(nb, dec_seq, DIFF_HEADS, 2, DIFF_DK))
        v_s.append(v_s_l.reshape(nb, dec_seq, DIFF_HEADS, DIFF_DV))
        s_s.append(s_new)

    k_prompt = jnp.stack(kp).reshape(depth, batch, DIFF_HEADS, 2, DIFF_DK, seq).transpose(0, 1, 5, 2, 3, 4)
    v_prompt = jnp.stack(vp).reshape(depth, batch, seq, DIFF_HEADS, DIFF_DV)
    return (yp.reshape(batch, seq, d), ys.reshape(nb, dec_seq, d), k_prompt, v_prompt, jnp.stack(sp),
            jnp.stack(k_s), jnp.stack(v_s), jnp.stack(s_s))
```

```python
import functools
import math

import jax
import jax.numpy as jnp
from jax import lax
from jax.experimental import pallas as pl
from jax.experimental.pallas import tpu as pltpu

f32 = jnp.float32
bf16 = jnp.bfloat16

DIFF_HEADS = 4
DIFF_DK = 64
DIFF_DV = 128
GLA_HEADS = 4
GLA_DK = 64
GLA_DV = 128
GLA_GATE_NORM = 16.0
ROPE_THETA = 10000.0
EPS = 1e-6
LOG2_E = math.log2(math.e)

QK_W = DIFF_HEADS * 2 * DIFF_DK
DV_W = DIFF_HEADS * DIFF_DV
GK_W = GLA_HEADS * GLA_DK
GV_W = GLA_HEADS * GLA_DV
MAIN_W = 2 * QK_W + DV_W + 2 * GK_W + 2 * GV_W

LANES = 128
GLA_SAMPLE_BLOCK = 32
GLA_BLOCK = 256
GLA_SAFE_LOG_DECAY = 80.0
VMEM_LIMIT = 56 * 1024 * 1024
FUSED_VMEM_LIMIT = 60 * 1024 * 1024


def _cparams(sem):
    return pltpu.CompilerParams(dimension_semantics=sem, vmem_limit_bytes=VMEM_LIMIT)


def _const_spec(shape):
    nd = len(shape)
    return pl.BlockSpec(shape, lambda *_: (0,) * nd, pipeline_mode=pl.Buffered(1))


def _layer_spec(shape, layer):
    nd = len(shape)
    return pl.BlockSpec((1,) + tuple(shape[1:]), lambda *_: (layer,) + (0,) * (nd - 1), pipeline_mode=pl.Buffered(1))


def _group_scale(z, width):
    lane = lax.broadcasted_iota(jnp.int32, (z.shape[0], LANES), 1)
    cols = []
    for c in range(z.shape[1] // LANES):
        zc = z[:, c * LANES:(c + 1) * LANES]
        zz = zc * zc
        if width == LANES:
            cols.append(jnp.broadcast_to(lax.rsqrt(jnp.mean(zz, axis=-1, keepdims=True) + EPS), zc.shape))
        else:
            lo = lane < width
            s_lo = jnp.sum(jnp.where(lo, zz, 0.0), axis=-1, keepdims=True)
            s_hi = jnp.sum(jnp.where(lo, 0.0, zz), axis=-1, keepdims=True)
            r_lo = lax.rsqrt(s_lo * (1.0 / width) + EPS)
            r_hi = lax.rsqrt(s_hi * (1.0 / width) + EPS)
            cols.append(jnp.where(lo, r_lo, r_hi))
    return jnp.concatenate(cols, axis=-1)


def _rope(z, cos, sin_signed):
    lane = lax.broadcasted_iota(jnp.int32, (z.shape[0], LANES), 1)
    first_half = (lane % DIFF_DK) < (DIFF_DK // 2)
    cols = []
    for c in range(z.shape[1] // LANES):
        zc = z[:, c * LANES:(c + 1) * LANES]
        partner = jnp.where(first_half, pltpu.roll(zc, LANES - DIFF_DK // 2, 1), pltpu.roll(zc, DIFF_DK // 2, 1))
        cols.append(zc * cos + partner * sin_signed)
    return jnp.concatenate(cols, axis=-1)


def _diff_lambda(lqk, lam_init):
    a = jnp.sum(lqk[0:1, :] * lqk[1:2, :], axis=-1, keepdims=True)
    b = jnp.sum(lqk[2:3, :] * lqk[3:4, :], axis=-1, keepdims=True)
    return jnp.exp(a) - jnp.exp(b) + lam_init


def _proj_body(x, g1_ref, w_ref, wr_ref, wa2_ref, ba_ref, qn_ref, kn_ref, cos, sin):
    n = x * lax.rsqrt(jnp.mean(x * x, axis=-1, keepdims=True) + EPS) * g1_ref[...]
    nb = n.astype(bf16)

    def seg(lo, width):
        return jnp.dot(nb, w_ref[0, :, lo:lo + width], preferred_element_type=f32)

    zq = seg(0, QK_W)
    q = _rope(zq * _group_scale(zq, DIFF_DK) * qn_ref[...], cos, sin) * (DIFF_DK ** -0.5 * LOG2_E)
    zk = seg(QK_W, QK_W)
    k = _rope(zk * _group_scale(zk, DIFF_DK) * kn_ref[...], cos, sin)
    v = seg(2 * QK_W, DV_W)
    off = 2 * QK_W + DV_W
    gq = seg(off, GK_W) * (GLA_DK ** -0.5)
    gk = seg(off + GK_W, GK_W)
    gv = seg(off + 2 * GK_W, GV_W)
    gg = seg(off + 2 * GK_W + GV_W, GV_W)
    r = jnp.dot(nb, wr_ref[0], preferred_element_type=f32)
    a = jnp.dot(r.astype(bf16), wa2_ref[0], preferred_element_type=f32) + ba_ref[...]
    gl = (jnp.minimum(a, 0.0) - jnp.log1p(jnp.exp(-jnp.abs(a)))) * (1.0 / GLA_GATE_NORM)
    return q, k, v, gq, gk, gv, gl, gg


def _proj_prompt_kernel(x_ref, g1_ref, w_ref, wr_ref, wa2_ref, ba_ref, qn_ref, kn_ref, cos_ref, sin_ref, *rest):
    q_ref, kb_ref, kt_ref, vb_ref, v4_ref, gq_ref, gk_ref, gv_ref, gl_ref, gg_ref = rest[-10:]
    layer = kt_ref.shape[0] - 1
    if layer:
        prev_kt_ref, prev_v4_ref = rest[:2]
        kt_ref[:layer] = prev_kt_ref[...]
        v4_ref[:layer] = prev_v4_ref[...]
    tm = x_ref.shape[0]
    q, k, v, gq, gk, gv, gl, gg = _proj_body(x_ref[...], g1_ref, w_ref, wr_ref, wa2_ref, ba_ref, qn_ref, kn_ref,
                                             cos_ref[...], sin_ref[...])
    first_map = lax.broadcasted_iota(jnp.int32, (tm, LANES), 1) < DIFF_DK
    for h in range(DIFF_HEADS):
        qh = q[:, h * LANES:(h + 1) * LANES]
        q_ref[:, (2 * h) * LANES:(2 * h + 1) * LANES] = jnp.where(first_map, qh, 0.0).astype(bf16)
        q_ref[:, (2 * h + 1) * LANES:(2 * h + 2) * LANES] = jnp.where(first_map, 0.0, qh).astype(bf16)
    kb_ref[...] = k.astype(bf16)
    kt_ref[layer, 0] = k.T
    vb_ref[...] = v.astype(bf16)
    for h in range(DIFF_HEADS):
        v4_ref[layer, pl.ds(h, tm, stride=DIFF_HEADS), :] = v[:, h * DIFF_DV:(h + 1) * DIFF_DV]
    gq_ref[...] = gq
    gk_ref[...] = gk
    gv_ref[...] = gv.astype(bf16)
    gl_ref[...] = gl
    gg_ref[...] = gg


def _proj_prompt(x2d, g1, w_main, w_r, w_a2, b_a, qn, kn, cos_t, sin_t, *, layer, batch, tm, prev_kv):
    n, d = x2d.shape
    seq = n // batch
    nt = seq // tm
    row = lambda w: pl.BlockSpec((tm, w), lambda b, j: (b * nt + j, 0))
    tab = pl.BlockSpec((tm, LANES), lambda b, j: (j, 0))
    arr = lambda w, dt: jax.ShapeDtypeStruct((n, w), dt)
    kt_spec = lambda nl: pl.BlockSpec((nl, 1, QK_W, tm), lambda b, j: (0, b, 0, j))
    v4_spec = lambda nl: pl.BlockSpec((nl, tm * DIFF_HEADS, DIFF_DV), lambda b, j: (0, b * nt + j, 0))
    out_shape = (
        arr(2 * QK_W, bf16),
        arr(QK_W, bf16),
        jax.ShapeDtypeStruct((layer + 1, batch, QK_W, seq), f32),
        arr(DV_W, bf16),
        jax.ShapeDtypeStruct((layer + 1, n * DIFF_HEADS, DIFF_DV), f32),
        arr(GK_W, f32), arr(GK_W, f32), arr(GV_W, bf16), arr(GK_W, f32), arr(GV_W, f32),
    )
    out_specs = (row(2 * QK_W), row(QK_W), kt_spec(layer + 1), row(DV_W), v4_spec(layer + 1),
                 row(GK_W), row(GK_W), row(GV_W), row(GK_W), row(GV_W))
    prev = () if prev_kv is None else tuple(prev_kv)
    return pl.pallas_call(
        _proj_prompt_kernel,
        grid=(batch, nt),
        in_specs=[row(d), _const_spec((1, d)), _layer_spec(w_main.shape, layer), _layer_spec(w_r.shape, layer),
                  _layer_spec(w_a2.shape, layer), _const_spec((1, GK_W)), _const_spec((1, QK_W)),
                  _const_spec((1, QK_W)), tab, tab] + ([kt_spec(layer), v4_spec(layer)] if prev else []),
        out_specs=out_specs,
        out_shape=out_shape,
        compiler_params=_cparams(("parallel", "parallel")),
        name="proj_prompt",
    )(x2d, g1, w_main, w_r, w_a2, b_a, qn, kn, cos_t, sin_t, *prev)


def _proj_sample_kernel(x_ref, g1_ref, w_ref, wr_ref, wa2_ref, ba_ref, qn_ref, kn_ref, cos_ref, sin_ref,
                        q_ref, k_ref, v_ref, gqt_ref, gkt_ref, gv_ref, et_ref, gg_ref):
    q, k, v, gq, gk, gv, gl, gg = _proj_body(x_ref[...], g1_ref, w_ref, wr_ref, wa2_ref, ba_ref, qn_ref, kn_ref,
                                             cos_ref[...], sin_ref[...])
    q_ref[...] = q
    k_ref[...] = k
    v_ref[...] = v
    gv_ref[...] = gv
    gg_ref[...] = gg
    e = jnp.exp(gl)
    tb = gqt_ref.shape[2]
    for blk in range(gqt_ref.shape[0]):
        rows = slice(blk * tb, (blk + 1) * tb)
        gqt_ref[blk] = gq[rows, :].T
        gkt_ref[blk] = gk[rows, :].T
        et_ref[blk] = e[rows, :].T


def _proj_sample(x2d, g1, w_main, w_r, w_a2, b_a, qn, kn, cos_t, sin_t, *, layer, tb):
    n, d = x2d.shape
    flat = lambda w: jax.ShapeDtypeStruct((n, w), f32)
    tr = jax.ShapeDtypeStruct((n // tb, GK_W, tb), f32)
    out_shape = (flat(QK_W), flat(QK_W), flat(DV_W), tr, tr, flat(GV_W), tr, flat(GV_W))
    whole = lambda shape: pl.BlockSpec(shape, lambda i: (0,) * len(shape))
    return pl.pallas_call(
        _proj_sample_kernel,
        grid=(1,),
        in_specs=[whole(x2d.shape), whole((1, d)), _layer_spec(w_main.shape, layer), _layer_spec(w_r.shape, layer),
                  _layer_spec(w_a2.shape, layer), whole((1, GK_W)), whole((1, QK_W)), whole((1, QK_W)),
                  whole(cos_t.shape), whole(sin_t.shape)],
        out_specs=tuple(whole(o.shape) for o in out_shape),
        out_shape=out_shape,
        compiler_params=_cparams(("arbitrary",)),
        name="proj_sample",
    )(x2d, g1, w_main, w_r, w_a2, b_a, qn, kn, cos_t, sin_t)


def _attn_kernel(q_ref, kb_ref, v_ref, lqk_ref, o_ref, vb_ref, s_ref, *, tq, heads, lam_init):
    seq = kb_ref.shape[0]
    lam = _diff_lambda(lqk_ref[...], lam_init)
    row = lax.broadcasted_iota(jnp.int32, (2 * tq, tq), 0) % tq
    col = lax.broadcasted_iota(jnp.int32, (2 * tq, tq), 1)
    visible = col <= row
    nq = seq // tq
    for hh in range(heads):
        kcols = slice(hh * LANES, (hh + 1) * LANES)
        vb_ref[hh, :, :DIFF_DV] = v_ref[:, kcols]
        vb_ref[hh, :, DIFF_DV:] = jnp.ones((seq, LANES), bf16)
        for qi in (range(nq) if hh % 2 == 0 else reversed(range(nq))):
            rows = slice(qi * tq, (qi + 1) * tq)
            qs = jnp.concatenate([q_ref[rows, (2 * hh) * LANES:(2 * hh + 1) * LANES],
                                  q_ref[rows, (2 * hh + 1) * LANES:(2 * hh + 2) * LANES]], axis=0)
            mx = None
            for kt in range(qi + 1):
                cols = slice(kt * tq, (kt + 1) * tq)
                s = lax.dot_general(qs, kb_ref[cols, kcols], (((1,), (1,)), ((), ())), preferred_element_type=f32)
                if kt == qi:
                    s = jnp.where(visible, s, -jnp.inf)
                s_ref[hh, :, cols] = s
                for c in range(tq // LANES):
                    sc = s[:, c * LANES:(c + 1) * LANES]
                    mx = sc if mx is None else jnp.maximum(mx, sc)
            m = jnp.broadcast_to(jnp.max(mx, axis=-1, keepdims=True), (2 * tq, tq))
            acc = None
            for kt in range(qi + 1):
                cols = slice(kt * tq, (kt + 1) * tq)
                p = jnp.exp2(s_ref[hh, :, cols] - m).astype(bf16)
                pv = jnp.dot(p, vb_ref[hh, cols, :], preferred_element_type=f32)
                acc = pv if acc is None else acc + pv
            o1 = acc[:tq, :DIFF_DV] / acc[:tq, DIFF_DV:]
            o2 = acc[tq:, :DIFF_DV] / acc[tq:, DIFF_DV:]
            o_ref[rows, hh * DIFF_DV:(hh + 1) * DIFF_DV] = o1 - lam * o2


def _attn_prompt(q, kb, vb, lqk, *, batch, seq, tq, heads, lam_init):
    return pl.pallas_call(
        functools.partial(_attn_kernel, tq=tq, heads=heads, lam_init=lam_init),
        grid=(batch, DIFF_HEADS // heads),
        in_specs=[pl.BlockSpec((seq, heads * 2 * LANES), lambda b, g: (b, g)),
                  pl.BlockSpec((seq, heads * LANES), lambda b, g: (b, g)),
                  pl.BlockSpec((seq, heads * LANES), lambda b, g: (b, g)),
                  _const_spec(lqk.shape)],
        out_specs=pl.BlockSpec((seq, heads * DIFF_DV), lambda b, g: (b, g)),
        out_shape=jax.ShapeDtypeStruct((batch * seq, DV_W), f32),
        scratch_shapes=[pltpu.VMEM((heads, seq, 2 * LANES), bf16), pltpu.VMEM((heads, 2 * tq, seq), f32)],
        compiler_params=_cparams(("parallel", "parallel")),
        name="attn_prompt",
    )(q, kb, vb, lqk)


def _cumsum_rows(g, tril):
    g1 = g.astype(bf16)
    r1 = g - g1.astype(f32)
    g2 = r1.astype(bf16)
    g3 = (r1 - g2.astype(f32)).astype(bf16)
    dot = lambda t: jnp.dot(tril, t, preferred_element_type=f32)
    return dot(g1) + dot(g2) + dot(g3)


def _gla_kernel(q_ref, k_ref, v_ref, g_ref, o_ref, s_ref, st_ref, b_ref, *, tg):
    step = pl.program_id(1)

    @pl.when(step == 0)
    def _():
        st_ref[...] = jnp.zeros(st_ref.shape, f32)

    c = GLA_BLOCK
    nblock = tg // c
    ri = lax.broadcasted_iota(jnp.int32, (c, c), 0)
    ci = lax.broadcasted_iota(jnp.int32, (c, c), 1)
    causal = ri >= ci
    tril = jnp.where(causal, 1.0, 0.0).astype(bf16)
    head0 = lax.broadcasted_iota(jnp.int32, (c, LANES), 1) < GLA_DK
    lane_sq = lax.broadcasted_iota(jnp.int32, (LANES, LANES), 1) < GLA_DK
    block_rows = lambda ib: slice(ib * c, (ib + 1) * c)
    head_cols = lambda h: slice(h * GLA_DV, (h + 1) * GLA_DV)
    head_mask = lambda j: head0 if j == 0 else jnp.logical_not(head0)

    worst = None
    for ib in range(nblock):
        b = _cumsum_rows(g_ref[block_rows(ib), :], tril)
        b_ref[block_rows(ib), :] = b
        b_mid = b[c // 2 - 1:c // 2, :]
        w = jnp.maximum(-b_mid, b_mid - b[c - 1:c, :])
        worst = w if worst is None else jnp.maximum(worst, w)
    safe = jnp.max(worst) < GLA_SAFE_LOG_DECAY

    @pl.when(safe)
    def _():
        for ib in range(nblock):
            rows = block_rows(ib)
            b = b_ref[rows, :]
            b_mid = b[c // 2 - 1:c // 2, :]
            qe = q_ref[rows, :] * jnp.exp(b - b_mid)
            ke = (k_ref[rows, :] * jnp.exp(b_mid - b)).astype(bf16)
            for h in range(GLA_HEADS):
                sl = slice((h // 2) * LANES, (h // 2 + 1) * LANES)
                qe_h = jnp.where(head_mask(h % 2), qe[:, sl], 0.0).astype(bf16)
                a = lax.dot_general(qe_h, ke[:, sl], (((1,), (1,)), ((), ())), preferred_element_type=f32)
                a = jnp.where(causal, a, 0.0).astype(bf16)
                o_ref[rows, head_cols(h)] = jnp.dot(a, v_ref[rows, head_cols(h)], preferred_element_type=f32)

    @pl.when(jnp.logical_not(safe))
    def _():
        tok = lax.broadcasted_iota(jnp.int32, (c, 1), 0)

        def exact_block(ib, carry):
            rows = pl.ds(pl.multiple_of(ib * c, c), c)
            b = b_ref[rows, :]
            q = q_ref[rows, :]
            k = k_ref[rows, :]
            v = v_ref[rows, :].astype(f32)
            pick = lambda x, s: jnp.sum(jnp.where(tok == s, x, 0.0), axis=0, keepdims=True)

            def key(s, acc):
                w = q * pick(k, s) * jnp.exp(jnp.minimum(b - pick(b, s), 0.0))
                w = jnp.where(tok >= s, w, 0.0)
                vs = pick(v, s)
                return tuple(
                    acc[h] + jnp.sum(w[:, h * GLA_DK:(h + 1) * GLA_DK], axis=-1, keepdims=True) * vs[:, head_cols(h)]
                    for h in range(GLA_HEADS))

            acc = lax.fori_loop(0, c, key, tuple(jnp.zeros((c, GLA_DV), f32) for _ in range(GLA_HEADS)))
            for h in range(GLA_HEADS):
                o_ref[rows, head_cols(h)] = acc[h]
            return carry

        lax.fori_loop(0, nblock, exact_block, 0)

    for ib in range(nblock):
        rows = block_rows(ib)
        b = b_ref[rows, :]
        b_last = b[c - 1:c, :]
        qb = q_ref[rows, :] * jnp.exp(b)
        kd = (k_ref[rows, :] * jnp.exp(b_last - b)).astype(bf16)
        e_last = jnp.exp(b_last)
        for p in range(GLA_HEADS // 2):
            sl = slice(p * LANES, (p + 1) * LANES)
            st = st_ref[p]
            stb = st.astype(bf16)
            upd = []
            for j in range(2):
                h = 2 * p + j
                qb_h = jnp.where(head_mask(j), qb[:, sl], 0.0).astype(bf16)
                o_ref[rows, head_cols(h)] += lax.dot_general(qb_h, stb, (((1,), (1,)), ((), ())),
                                                             preferred_element_type=f32)
                upd.append(lax.dot_general(v_ref[rows, head_cols(h)], kd[:, sl], (((0,), (0,)), ((), ())),
                                           preferred_element_type=f32))
            st_ref[p] = st * e_last[:, sl] + jnp.where(lane_sq, upd[0], upd[1])

    @pl.when(step == pl.num_programs(1) - 1)
    def _():
        for p in range(GLA_HEADS // 2):
            t = st_ref[p].T
            s_ref[0, 2 * p] = t[:GLA_DK, :]
            s_ref[0, 2 * p + 1] = t[GLA_DK:, :]


def _gla_prompt(gq, gk, gv, gl, *, batch, seq, tg):
    ns = seq // tg
    row = lambda w: pl.BlockSpec((tg, w), lambda b, i: (b * ns + i, 0))
    return pl.pallas_call(
        functools.partial(_gla_kernel, tg=tg),
        grid=(batch, ns),
        in_specs=[row(GK_W), row(GK_W), row(GV_W), row(GK_W)],
        out_specs=(row(GV_W), pl.BlockSpec((1, GLA_HEADS, GLA_DK, GLA_DV), lambda b, i: (b, 0, 0, 0))),
        out_shape=(jax.ShapeDtypeStruct((batch * seq, GV_W), f32),
                   jax.ShapeDtypeStruct((batch, GLA_HEADS, GLA_DK, GLA_DV), f32)),
        scratch_shapes=[pltpu.VMEM((GLA_HEADS // 2, LANES, LANES), f32), pltpu.VMEM((tg, GK_W), f32)],
        compiler_params=_cparams(("parallel", "arbitrary")),
        name="gla_prompt",
    )(gq, gk, gv, gl)


def _merge_ffn_pieces(x_ref, od_ref, og_ref, gg_ref, sub_ref, gn_ref, wo_ref, n2_ref, lam_init):
    od = od_ref[...]
    odn = od * _group_scale(od, DIFF_DV) * sub_ref[...] * (1.0 - lam_init)
    og = og_ref[...]
    ogn = og * _group_scale(og, GLA_DV) * gn_ref[...]
    gg = gg_ref[...]
    ogn = ogn * (gg * jax.nn.sigmoid(gg))
    mix = jnp.concatenate([odn, ogn], axis=-1).astype(bf16)
    y = x_ref[...] + jnp.dot(mix, wo_ref[0], preferred_element_type=f32)
    n2 = (y * lax.rsqrt(jnp.mean(y * y, axis=-1, keepdims=True) + EPS) * n2_ref[...]).astype(bf16)
    return y, n2


def _ffn_chunk(n2, wgu_ref, wd_ref, c0, ff_chunk, d_ff):
    a = jnp.dot(n2, wgu_ref[0, :, c0:c0 + ff_chunk], preferred_element_type=f32)
    b = jnp.dot(n2, wgu_ref[0, :, d_ff + c0:d_ff + c0 + ff_chunk], preferred_element_type=f32)
    hid = (a * jax.nn.sigmoid(a) * b).astype(bf16)
    return jnp.dot(hid, wd_ref[0, c0:c0 + ff_chunk, :], preferred_element_type=f32)


def _post_kernel(x_ref, od_ref, og_ref, gg_ref, sub_ref, gn_ref, wo_ref, n2_ref, wgu_ref, wd_ref, y_ref,
                 *, lam_init, d_ff, ff_chunk):
    acc, n2 = _merge_ffn_pieces(x_ref, od_ref, og_ref, gg_ref, sub_ref, gn_ref, wo_ref, n2_ref, lam_init)
    for c0 in range(0, d_ff, ff_chunk):
        acc = acc + _ffn_chunk(n2, wgu_ref, wd_ref, c0, ff_chunk, d_ff)
    y_ref[...] = acc


def _post(x2d, od, og, gg, sub, gn, w_out, n2, w_gu, w_down, *, layer, tm, lam_init):
    n, d = x2d.shape
    d_ff = w_down.shape[1]
    row = lambda w: pl.BlockSpec((tm, w), lambda i: (i, 0))
    return pl.pallas_call(
        functools.partial(_post_kernel, lam_init=lam_init, d_ff=d_ff, ff_chunk=2 * LANES),
        grid=(n // tm,),
        in_specs=[row(d), row(DV_W), row(GV_W), row(GV_W), _const_spec((1, DV_W)), _const_spec((1, GV_W)),
                  _layer_spec(w_out.shape, layer), _const_spec((1, d)), _layer_spec(w_gu.shape, layer),
                  _layer_spec(w_down.shape, layer)],
        out_specs=row(d),
        out_shape=jax.ShapeDtypeStruct((n, d), f32),
        compiler_params=_cparams(("parallel",)),
        name="post",
    )(x2d, od, og, gg, sub, gn, w_out, n2, w_gu, w_down)


def _sample_probs(q, k_new, kbuf, slot, *, n_pages, page):
    past = n_pages * page
    nrow = 2 * DIFF_HEADS
    qb = q.astype(bf16)
    rid = lax.broadcasted_iota(jnp.int32, (nrow, past), 0)
    s = jnp.zeros((nrow, past), f32)
    for h in range(DIFF_HEADS):
        for mp in range(2):
            kt = jnp.concatenate([kbuf[slot, j, h, mp] for j in range(n_pages)], axis=-1)
            sr = jnp.dot(qb, kt.astype(bf16), preferred_element_type=f32)
            s = jnp.where(rid == 2 * h + mp, sr, s)
    s_new = jnp.sum(q * k_new, axis=-1, keepdims=True)
    m = jnp.maximum(jnp.max(s, axis=-1, keepdims=True), s_new)
    p = jnp.exp2(s - m)
    p_new = jnp.exp2(s_new - m)
    inv_l = 1.0 / (jnp.sum(p, axis=-1, keepdims=True) + p_new)
    return p.astype(bf16), p_new, inv_l


def _sample_values(pb, p_new, inv_l, v_new, lam, vbuf, slot, *, n_pages, page):
    def head_values(h):
        vh = jnp.concatenate([vbuf[slot, j, pl.ds(h, page, stride=DIFF_HEADS), :] for j in range(n_pages)], axis=0)
        return vh.astype(bf16)

    heads = []
    for h0 in range(0, DIFF_HEADS, 2):
        pair = jnp.dot(pb, jnp.concatenate([head_values(h0), head_values(h0 + 1)], axis=-1),
                       preferred_element_type=f32)
        for h in (h0, h0 + 1):
            oh = pair[:, (h - h0) * DIFF_DV:(h - h0 + 1) * DIFF_DV]
            oh = (oh + p_new * v_new[:, h * DIFF_DV:(h + 1) * DIFF_DV]) * inv_l
            heads.append(oh[2 * h:2 * h + 1, :] - lam * oh[2 * h + 1:2 * h + 2, :])
    return jnp.concatenate(heads, axis=-1)


def _page_copies(pt_ref, ck_hbm, cv_hbm, kbuf, vbuf, sem, sample, slot, *, layer, n_pages):
    out = []
    for j in range(n_pages):
        phys = pt_ref[sample * n_pages + j]
        out.append(pltpu.make_async_copy(ck_hbm.at[layer, phys], kbuf.at[slot, j], sem.at[0, slot]))
        out.append(pltpu.make_async_copy(cv_hbm.at[layer, phys], vbuf.at[slot, j], sem.at[1, slot]))
    return out


def _post_attn_kernel(pt_ref, x_ref, od_ref, og_ref, gg_ref, sub_ref, gn_ref, wo_ref, n2_ref, wgu_ref, wd_ref,
                      q_ref, kn_ref, vn_ref, lqk_ref, ck_hbm, cv_hbm, y_ref, os_ref, kbuf, vbuf, sem,
                      *, lam_init, d_ff, ff_chunk, layer, n_pages, page, per_step):
    step = pl.program_id(0)
    nsteps = pl.num_programs(0)
    copies = functools.partial(_page_copies, pt_ref, ck_hbm, cv_hbm, kbuf, vbuf, sem, layer=layer, n_pages=n_pages)

    @pl.when(step == 0)
    def _():
        for cp in copies(0, 0):
            cp.start()

    lam = _diff_lambda(lqk_ref[...], lam_init)

    def sample_probs(j):
        g = step * per_step + j
        slot = j % 2
        if j + 1 < per_step:
            for cp in copies(g + 1, 1 - slot):
                cp.start()
        else:
            @pl.when(step + 1 < nsteps)
            def _():
                for cp in copies(g + 1, 1 - slot):
                    cp.start()
        for cp in copies(g, slot):
            cp.wait()
        return _sample_probs(q_ref[j], kn_ref[j], kbuf, slot, n_pages=n_pages, page=page)

    def sample_values(j, probs):
        os_ref[j] = _sample_values(*probs, vn_ref[j], lam, vbuf, j % 2, n_pages=n_pages, page=page)

    acc, n2 = _merge_ffn_pieces(x_ref, od_ref, og_ref, gg_ref, sub_ref, gn_ref, wo_ref, n2_ref, lam_init)
    nchunk = d_ff // ff_chunk
    sample_at = {(j * nchunk) // per_step: j for j in range(per_step)}
    for ic in range(nchunk):
        if ic in sample_at:
            probs = sample_probs(sample_at[ic])
        acc = acc + _ffn_chunk(n2, wgu_ref, wd_ref, ic * ff_chunk, ff_chunk, d_ff)
        if ic in sample_at:
            sample_values(sample_at[ic], probs)
    y_ref[...] = acc


def _post_attn(x2d, od, og, gg, sub, gn, w_out, n2, w_gu, w_down, page_table, q_s, kn_s, vn_s, lqk, cache_kt, cache_vr,
               *, tm, lam_init, layer):
    n, d = x2d.shape
    d_ff = w_down.shape[1]
    nsteps = n // tm
    nb, n_pages = page_table.shape
    per_step = nb // nsteps
    assert per_step * nsteps == nb and per_step % 2 == 0
    page = cache_kt.shape[-1]
    nrow = 2 * DIFF_HEADS
    row = lambda w: pl.BlockSpec((tm, w), lambda i, pt: (i, 0))
    const = lambda shape: pl.BlockSpec(shape, lambda i, pt: (0,) * len(shape), pipeline_mode=pl.Buffered(1))
    per = lambda r, w: pl.BlockSpec((per_step, r, w), lambda i, pt: (i, 0, 0))
    grid_spec = pltpu.PrefetchScalarGridSpec(
        num_scalar_prefetch=1,
        grid=(nsteps,),
        in_specs=[row(d), row(DV_W), row(GV_W), row(GV_W), const((1, DV_W)), const((1, GV_W)),
                  _layer_spec(w_out.shape, layer), const((1, d)), _layer_spec(w_gu.shape, layer),
                  _layer_spec(w_down.shape, layer),
                  per(nrow, DIFF_DK), per(nrow, DIFF_DK), per(1, DV_W), const(lqk.shape),
                  pl.BlockSpec(memory_space=pl.ANY), pl.BlockSpec(memory_space=pl.ANY)],
        out_specs=(row(d), per(1, DV_W)),
        scratch_shapes=[pltpu.VMEM((2, n_pages) + cache_kt.shape[2:], f32),
                        pltpu.VMEM((2, n_pages) + cache_vr.shape[2:], f32),
                        pltpu.SemaphoreType.DMA((2, 2))],
    )
    y, o_s = pl.pallas_call(
        functools.partial(_post_attn_kernel, lam_init=lam_init, d_ff=d_ff, ff_chunk=2 * LANES, layer=layer,
                          n_pages=n_pages, page=page, per_step=per_step),
        grid_spec=grid_spec,
        out_shape=(jax.ShapeDtypeStruct((n, d), f32), jax.ShapeDtypeStruct((nb, 1, DV_W), f32)),
        compiler_params=pltpu.CompilerParams(dimension_semantics=("arbitrary",), vmem_limit_bytes=FUSED_VMEM_LIMIT),
        name="post_attn",
    )(page_table.reshape(-1), x2d, od, og, gg, sub, gn, w_out, n2, w_gu, w_down,
      q_s.reshape(nb, nrow, DIFF_DK), kn_s.reshape(nb, nrow, DIFF_DK), vn_s.reshape(nb, 1, DV_W), lqk,
      cache_kt, cache_vr)
    return y, o_s.reshape(nb, DV_W)


def _gla_sample_kernel(qt_ref, kt_ref, v_ref, et_ref, s0_ref, o_ref, s_ref, *, tb):
    qt = qt_ref[0]
    kt = kt_ref[0]
    et = et_ref[0]
    v = v_ref[...]
    for i in range(tb):
        for h in range(GLA_HEADS):
            ks = slice(h * GLA_DK, (h + 1) * GLA_DK)
            vs = slice(h * GLA_DV, (h + 1) * GLA_DV)
            s_new = s0_ref[0, i, h] * et[ks, i:i + 1] + kt[ks, i:i + 1] * v[i:i + 1, vs]
            s_ref[i, h] = s_new
            o_ref[i:i + 1, vs] = jnp.sum(qt[ks, i:i + 1] * s_new, axis=0, keepdims=True)


def _gla_sample(gqt, gkt, gv, et, s0_all, *, layer):
    nblk, _, tb = gqt.shape
    tr = pl.BlockSpec((1, GK_W, tb), lambda i: (i, 0, 0))
    row = pl.BlockSpec((tb, GV_W), lambda i: (i, 0))
    st_in = pl.BlockSpec((1, tb, GLA_HEADS, GLA_DK, GLA_DV), lambda i: (layer, i, 0, 0, 0))
    st = pl.BlockSpec((tb, GLA_HEADS, GLA_DK, GLA_DV), lambda i: (i, 0, 0, 0))
    return pl.pallas_call(
        functools.partial(_gla_sample_kernel, tb=tb),
        grid=(nblk,),
        in_specs=[tr, tr, row, tr, st_in],
        out_specs=(row, st),
        out_shape=(jax.ShapeDtypeStruct((nblk * tb, GV_W), f32), jax.ShapeDtypeStruct(s0_all.shape[1:], f32)),
        compiler_params=_cparams(("parallel",)),
        name="gla_sample",
    )(gqt, gkt, gv, et, s0_all)


def _rope_tables(pos):
    half = DIFF_DK // 2
    freqs = ROPE_THETA ** (-jnp.arange(half, dtype=f32) / half)
    ang = pos.astype(f32)[:, None] * freqs[None, :]
    cos, sin = jnp.cos(ang), jnp.sin(ang)
    reps = LANES // DIFF_DK
    return jnp.tile(jnp.concatenate([cos, cos], -1), (1, reps)), jnp.tile(jnp.concatenate([-sin, sin], -1), (1, reps))


def kernel(x_prompt, x_sample, cache_k, cache_v, state_gla, page_table, norm1, w_in, q_norm, k_norm, lambda_qk,
           subln, w_a2, b_a, gla_norm, w_out, norm2, w_gu, w_down):
    batch, seq, d = x_prompt.shape
    nb, dec_seq, _ = x_sample.shape
    assert dec_seq == 1
    depth = w_in.shape[0]
    n_pages = page_table.shape[1]
    page = cache_k.shape[2]
    past = n_pages * page
    rank = w_a2.shape[1]

    cos_p, sin_p = _rope_tables(jnp.arange(seq))
    cos_s, sin_s = _rope_tables(past + jnp.arange(dec_seq))
    cos_s = jnp.broadcast_to(cos_s, (nb, LANES))
    sin_s = jnp.broadcast_to(sin_s, (nb, LANES))

    ck = jnp.transpose(cache_k, (0, 1, 3, 4, 5, 2))
    cv = cache_v.reshape(depth, -1, page * DIFF_HEADS, DIFF_DV)

    yp = x_prompt.reshape(batch * seq, d)
    ys = x_sample.reshape(nb, d)
    w_in_b = w_in.astype(bf16)
    w_r = jnp.pad(w_in[:, :, MAIN_W:], ((0, 0), (0, 0), (0, LANES - rank))).astype(bf16)
    wa2 = jnp.pad(w_a2, ((0, 0), (0, LANES - rank), (0, 0))).astype(bf16)
    w_out_b, w_gu_b, w_down_b = w_out.astype(bf16), w_gu.astype(bf16), w_down.astype(bf16)

    sp, k_s, v_s, s_s = [], [], [], []
    kv_prompt = None
    for l in range(depth):
        lam_init = 0.8 - 0.6 * math.exp(-0.3 * l)
        proj_args = (norm1[l][None], w_in_b, w_r, wa2, b_a[l][None],
                     jnp.tile(q_norm[l], QK_W // DIFF_DK)[None], jnp.tile(k_norm[l], QK_W // DIFF_DK)[None])
        post_args = (jnp.tile(subln[l], DIFF_HEADS)[None], jnp.tile(gla_norm[l], GLA_HEADS)[None],
                     w_out_b, norm2[l][None], w_gu_b, w_down_b)

        q, kb, kt, vb, v4, gq, gk, gv, gl, gg = _proj_prompt(yp, *proj_args, cos_p, sin_p, layer=l, batch=batch,
                                                             tm=512, prev_kv=kv_prompt)
        kv_prompt = (kt, v4)
        q_s, k_s_l, v_s_l, gqt, gkt, gv_s, et, gg_s = _proj_sample(ys, *proj_args, cos_s, sin_s, layer=l,
                                                                   tb=GLA_SAMPLE_BLOCK)
        od = _attn_prompt(q, kb, vb, lambda_qk[l], batch=batch, seq=seq, tq=256, heads=2, lam_init=lam_init)
        og, s_fin = _gla_prompt(gq, gk, gv, gl, batch=batch, seq=seq, tg=512)
        yp, od_s = _post_attn(yp, od, og, gg, *post_args, page_table, q_s, k_s_l, v_s_l, lambda_qk[l], ck, cv,
                              tm=512, lam_init=lam_init, layer=l)
        sp.append(s_fin)

        og_s, s_new = _gla_sample(gqt, gkt, gv_s, et, state_gla, layer=l)
        ys = _post(ys, od_s, og_s, gg_s, *post_args, layer=l, tm=nb, lam_init=lam_init)
        k_s.append(k_s_l.reshape(nb, dec_seq, DIFF_HEADS, 2, DIFF_DK))
        v_s.append(v_s_l.reshape(nb, dec_seq, DIFF_HEADS, DIFF_DV))
        s_s.append(s_new)

    k_prompt = kv_prompt[0].reshape(depth, batch, DIFF_HEADS, 2, DIFF_DK, seq).transpose(0, 1, 5, 2, 3, 4)
    v_prompt = kv_prompt[1].reshape(depth, batch, seq, DIFF_HEADS, DIFF_DV)
    return (yp.reshape(batch, seq, d), ys.reshape(nb, dec_seq, d), k_prompt, v_prompt, jnp.stack(sp),
            jnp.stack(k_s), jnp.stack(v_s), jnp.stack(s_s))
```

```python
import functools
import math

import jax
import jax.numpy as jnp
from jax import lax
from jax.experimental import pallas as pl
from jax.experimental.pallas import tpu as pltpu

f32 = jnp.float32
bf16 = jnp.bfloat16

DIFF_HEADS = 4
DIFF_DK = 64
DIFF_DV = 128
GLA_HEADS = 4
GLA_DK = 64
GLA_DV = 128
GLA_GATE_NORM = 16.0
ROPE_THETA = 10000.0
EPS = 1e-6
LOG2_E = math.log2(math.e)

QK_W = DIFF_HEADS * 2 * DIFF_DK
DV_W = DIFF_HEADS * DIFF_DV
GK_W = GLA_HEADS * GLA_DK
GV_W = GLA_HEADS * GLA_DV
MAIN_W = 2 * QK_W + DV_W + 2 * GK_W + 2 * GV_W

LANES = 128
GLA_SAMPLE_BLOCK = 32
GLA_BLOCK = 256
GLA_SAFE_LOG_DECAY = 80.0
VMEM_LIMIT = 56 * 1024 * 1024
FUSED_VMEM_LIMIT = 60 * 1024 * 1024


def _cparams(sem):
    return pltpu.CompilerParams(dimension_semantics=sem, vmem_limit_bytes=VMEM_LIMIT)


def _const_spec(shape):
    nd = len(shape)
    return pl.BlockSpec(shape, lambda *_: (0,) * nd, pipeline_mode=pl.Buffered(1))


def _layer_spec(shape, layer):
    nd = len(shape)
    return pl.BlockSpec((1,) + tuple(shape[1:]), lambda *_: (layer,) + (0,) * (nd - 1), pipeline_mode=pl.Buffered(1))


def _group_scale(z, width):
    lane = lax.broadcasted_iota(jnp.int32, (z.shape[0], LANES), 1)
    cols = []
    for c in range(z.shape[1] // LANES):
        zc = z[:, c * LANES:(c + 1) * LANES]
        zz = zc * zc
        if width == LANES:
            cols.append(jnp.broadcast_to(lax.rsqrt(jnp.mean(zz, axis=-1, keepdims=True) + EPS), zc.shape))
        else:
            lo = lane < width
            s_lo = jnp.sum(jnp.where(lo, zz, 0.0), axis=-1, keepdims=True)
            s_hi = jnp.sum(jnp.where(lo, 0.0, zz), axis=-1, keepdims=True)
            r_lo = lax.rsqrt(s_lo * (1.0 / width) + EPS)
            r_hi = lax.rsqrt(s_hi * (1.0 / width) + EPS)
            cols.append(jnp.where(lo, r_lo, r_hi))
    return jnp.concatenate(cols, axis=-1)


def _rope(z, cos, sin_signed):
    lane = lax.broadcasted_iota(jnp.int32, (z.shape[0], LANES), 1)
    first_half = (lane % DIFF_DK) < (DIFF_DK // 2)
    cols = []
    for c in range(z.shape[1] // LANES):
        zc = z[:, c * LANES:(c + 1) * LANES]
        partner = jnp.where(first_half, pltpu.roll(zc, LANES - DIFF_DK // 2, 1), pltpu.roll(zc, DIFF_DK // 2, 1))
        cols.append(zc * cos + partner * sin_signed)
    return jnp.concatenate(cols, axis=-1)


def _diff_lambda(lqk, lam_init):
    a = jnp.sum(lqk[0:1, :] * lqk[1:2, :], axis=-1, keepdims=True)
    b = jnp.sum(lqk[2:3, :] * lqk[3:4, :], axis=-1, keepdims=True)
    return jnp.exp(a) - jnp.exp(b) + lam_init


def _proj_body(x, g1_ref, w_ref, wr_ref, wa2_ref, ba_ref, qn_ref, kn_ref, cos, sin):
    n = x * lax.rsqrt(jnp.mean(x * x, axis=-1, keepdims=True) + EPS) * g1_ref[...]
    nb = n.astype(bf16)

    def seg(lo, width):
        return jnp.dot(nb, w_ref[0, :, lo:lo + width], preferred_element_type=f32)

    zq = seg(0, QK_W)
    q = _rope(zq * _group_scale(zq, DIFF_DK) * qn_ref[...], cos, sin) * (DIFF_DK ** -0.5 * LOG2_E)
    zk = seg(QK_W, QK_W)
    k = _rope(zk * _group_scale(zk, DIFF_DK) * kn_ref[...], cos, sin)
    v = seg(2 * QK_W, DV_W)
    off = 2 * QK_W + DV_W
    gq = seg(off, GK_W) * (GLA_DK ** -0.5)
    gk = seg(off + GK_W, GK_W)
    gv = seg(off + 2 * GK_W, GV_W)
    gg = seg(off + 2 * GK_W + GV_W, GV_W)
    r = jnp.dot(nb, wr_ref[0], preferred_element_type=f32)
    a = jnp.dot(r.astype(bf16), wa2_ref[0], preferred_element_type=f32) + ba_ref[...]
    gl = (jnp.minimum(a, 0.0) - jnp.log1p(jnp.exp(-jnp.abs(a)))) * (1.0 / GLA_GATE_NORM)
    return q, k, v, gq, gk, gv, gl, gg


def _proj_prompt_kernel(x_ref, g1_ref, w_ref, wr_ref, wa2_ref, ba_ref, qn_ref, kn_ref, cos_ref, sin_ref, *rest):
    q_ref, kb_ref, kt_ref, vb_ref, v4_ref, gq_ref, gk_ref, gv_ref, gl_ref, gg_ref = rest[-10:]
    layer = kt_ref.shape[0] - 1
    if layer:
        prev_kt_ref, prev_v4_ref = rest[:2]
        kt_ref[:layer] = prev_kt_ref[...]
        v4_ref[:layer] = prev_v4_ref[...]
    tm = x_ref.shape[0]
    q, k, v, gq, gk, gv, gl, gg = _proj_body(x_ref[...], g1_ref, w_ref, wr_ref, wa2_ref, ba_ref, qn_ref, kn_ref,
                                             cos_ref[...], sin_ref[...])
    first_map = lax.broadcasted_iota(jnp.int32, (tm, LANES), 1) < DIFF_DK
    for h in range(DIFF_HEADS):
        qh = q[:, h * LANES:(h + 1) * LANES]
        q_ref[:, (2 * h) * LANES:(2 * h + 1) * LANES] = jnp.where(first_map, qh, 0.0).astype(bf16)
        q_ref[:, (2 * h + 1) * LANES:(2 * h + 2) * LANES] = jnp.where(first_map, 0.0, qh).astype(bf16)
    kb_ref[...] = k.astype(bf16)
    kt_ref[layer, 0] = k.T
    vb_ref[...] = v.astype(bf16)
    for h in range(DIFF_HEADS):
        v4_ref[layer, pl.ds(h, tm, stride=DIFF_HEADS), :] = v[:, h * DIFF_DV:(h + 1) * DIFF_DV]
    gq_ref[...] = gq
    gk_ref[...] = gk
    gv_ref[...] = gv.astype(bf16)
    gl_ref[...] = gl
    gg_ref[...] = gg


def _proj_prompt(x2d, g1, w_main, w_r, w_a2, b_a, qn, kn, cos_t, sin_t, *, layer, batch, tm, prev_kv):
    n, d = x2d.shape
    seq = n // batch
    nt = seq // tm
    row = lambda w: pl.BlockSpec((tm, w), lambda b, j: (b * nt + j, 0))
    tab = pl.BlockSpec((tm, LANES), lambda b, j: (j, 0))
    arr = lambda w, dt: jax.ShapeDtypeStruct((n, w), dt)
    kt_spec = lambda nl: pl.BlockSpec((nl, 1, QK_W, tm), lambda b, j: (0, b, 0, j))
    v4_spec = lambda nl: pl.BlockSpec((nl, tm * DIFF_HEADS, DIFF_DV), lambda b, j: (0, b * nt + j, 0))
    out_shape = (
        arr(2 * QK_W, bf16),
        arr(QK_W, bf16),
        jax.ShapeDtypeStruct((layer + 1, batch, QK_W, seq), f32),
        arr(DV_W, bf16),
        jax.ShapeDtypeStruct((layer + 1, n * DIFF_HEADS, DIFF_DV), f32),
        arr(GK_W, f32), arr(GK_W, f32), arr(GV_W, bf16), arr(GK_W, f32), arr(GV_W, f32),
    )
    out_specs = (row(2 * QK_W), row(QK_W), kt_spec(layer + 1), row(DV_W), v4_spec(layer + 1),
                 row(GK_W), row(GK_W), row(GV_W), row(GK_W), row(GV_W))
    prev = () if prev_kv is None else tuple(prev_kv)
    return pl.pallas_call(
        _proj_prompt_kernel,
        grid=(batch, nt),
        in_specs=[row(d), _const_spec((1, d)), _layer_spec(w_main.shape, layer), _layer_spec(w_r.shape, layer),
                  _layer_spec(w_a2.shape, layer), _const_spec((1, GK_W)), _const_spec((1, QK_W)),
                  _const_spec((1, QK_W)), tab, tab] + ([kt_spec(layer), v4_spec(layer)] if prev else []),
        out_specs=out_specs,
        out_shape=out_shape,
        compiler_params=_cparams(("parallel", "parallel")),
        name="proj_prompt",
    )(x2d, g1, w_main, w_r, w_a2, b_a, qn, kn, cos_t, sin_t, *prev)


def _proj_sample_kernel(x_ref, g1_ref, w_ref, wr_ref, wa2_ref, ba_ref, qn_ref, kn_ref, cos_ref, sin_ref,
                        q_ref, k_ref, v_ref, gqt_ref, gkt_ref, gv_ref, et_ref, gg_ref):
    q, k, v, gq, gk, gv, gl, gg = _proj_body(x_ref[...], g1_ref, w_ref, wr_ref, wa2_ref, ba_ref, qn_ref, kn_ref,
                                             cos_ref[...], sin_ref[...])
    q_ref[...] = q
    k_ref[...] = k
    v_ref[...] = v
    gv_ref[...] = gv
    gg_ref[...] = gg
    e = jnp.exp(gl)
    tb = gqt_ref.shape[2]
    for blk in range(gqt_ref.shape[0]):
        rows = slice(blk * tb, (blk + 1) * tb)
        gqt_ref[blk] = gq[rows, :].T
        gkt_ref[blk] = gk[rows, :].T
        et_ref[blk] = e[rows, :].T


def _proj_sample(x2d, g1, w_main, w_r, w_a2, b_a, qn, kn, cos_t, sin_t, *, layer, tb):
    n, d = x2d.shape
    flat = lambda w: jax.ShapeDtypeStruct((n, w), f32)
    tr = jax.ShapeDtypeStruct((n // tb, GK_W, tb), f32)
    out_shape = (flat(QK_W), flat(QK_W), flat(DV_W), tr, tr, flat(GV_W), tr, flat(GV_W))
    whole = lambda shape: pl.BlockSpec(shape, lambda i: (0,) * len(shape))
    return pl.pallas_call(
        _proj_sample_kernel,
        grid=(1,),
        in_specs=[whole(x2d.shape), whole((1, d)), _layer_spec(w_main.shape, layer), _layer_spec(w_r.shape, layer),
                  _layer_spec(w_a2.shape, layer), whole((1, GK_W)), whole((1, QK_W)), whole((1, QK_W)),
                  whole(cos_t.shape), whole(sin_t.shape)],
        out_specs=tuple(whole(o.shape) for o in out_shape),
        out_shape=out_shape,
        compiler_params=_cparams(("arbitrary",)),
        name="proj_sample",
    )(x2d, g1, w_main, w_r, w_a2, b_a, qn, kn, cos_t, sin_t)


def _attn_kernel(q_ref, kb_ref, v_ref, lqk_ref, o_ref, vb_ref, s_ref, *, tq, heads, lam_init):
    seq = kb_ref.shape[0]
    lam = _diff_lambda(lqk_ref[...], lam_init)
    row = lax.broadcasted_iota(jnp.int32, (2 * tq, tq), 0) % tq
    col = lax.broadcasted_iota(jnp.int32, (2 * tq, tq), 1)
    visible = col <= row
    nq = seq // tq
    for hh in range(heads):
        kcols = slice(hh * LANES, (hh + 1) * LANES)
        vb_ref[hh, :, :DIFF_DV] = v_ref[:, kcols]
        vb_ref[hh, :, DIFF_DV:] = jnp.ones((seq, LANES), bf16)
        for qi in (range(nq) if hh % 2 == 0 else reversed(range(nq))):
            rows = slice(qi * tq, (qi + 1) * tq)
            qs = jnp.concatenate([q_ref[rows, (2 * hh) * LANES:(2 * hh + 1) * LANES],
                                  q_ref[rows, (2 * hh + 1) * LANES:(2 * hh + 2) * LANES]], axis=0)
            mx = None
            for kt in range(qi + 1):
                cols = slice(kt * tq, (kt + 1) * tq)
                s = lax.dot_general(qs, kb_ref[cols, kcols], (((1,), (1,)), ((), ())), preferred_element_type=f32)
                if kt == qi:
                    s = jnp.where(visible, s, -jnp.inf)
                s_ref[hh, :, cols] = s
                for c in range(tq // LANES):
                    sc = s[:, c * LANES:(c + 1) * LANES]
                    mx = sc if mx is None else jnp.maximum(mx, sc)
            m = jnp.broadcast_to(jnp.max(mx, axis=-1, keepdims=True), (2 * tq, tq))
            acc = None
            for kt in range(qi + 1):
                cols = slice(kt * tq, (kt + 1) * tq)
                p = jnp.exp2(s_ref[hh, :, cols] - m).astype(bf16)
                pv = jnp.dot(p, vb_ref[hh, cols, :], preferred_element_type=f32)
                acc = pv if acc is None else acc + pv
            o1 = acc[:tq, :DIFF_DV] / acc[:tq, DIFF_DV:]
            o2 = acc[tq:, :DIFF_DV] / acc[tq:, DIFF_DV:]
            o_ref[rows, hh * DIFF_DV:(hh + 1) * DIFF_DV] = o1 - lam * o2


def _attn_prompt(q, kb, vb, lqk, *, batch, seq, tq, heads, lam_init):
    return pl.pallas_call(
        functools.partial(_attn_kernel, tq=tq, heads=heads, lam_init=lam_init),
        grid=(batch, DIFF_HEADS // heads),
        in_specs=[pl.BlockSpec((seq, heads * 2 * LANES), lambda b, g: (b, g)),
                  pl.BlockSpec((seq, heads * LANES), lambda b, g: (b, g)),
                  pl.BlockSpec((seq, heads * LANES), lambda b, g: (b, g)),
                  _const_spec(lqk.shape)],
        out_specs=pl.BlockSpec((seq, heads * DIFF_DV), lambda b, g: (b, g)),
        out_shape=jax.ShapeDtypeStruct((batch * seq, DV_W), f32),
        scratch_shapes=[pltpu.VMEM((heads, seq, 2 * LANES), bf16), pltpu.VMEM((heads, 2 * tq, seq), f32)],
        compiler_params=_cparams(("parallel", "parallel")),
        name="attn_prompt",
    )(q, kb, vb, lqk)


def _cumsum_rows(g, tril):
    g1 = g.astype(bf16)
    r1 = g - g1.astype(f32)
    g2 = r1.astype(bf16)
    g3 = (r1 - g2.astype(f32)).astype(bf16)
    dot = lambda t: jnp.dot(tril, t, preferred_element_type=f32)
    return dot(g1) + dot(g2) + dot(g3)


def _gla_kernel(q_ref, k_ref, v_ref, g_ref, o_ref, s_ref, st_ref, b_ref, *, tg):
    step = pl.program_id(1)

    @pl.when(step == 0)
    def _():
        st_ref[...] = jnp.zeros(st_ref.shape, f32)

    c = GLA_BLOCK
    nblock = tg // c
    ri = lax.broadcasted_iota(jnp.int32, (c, c), 0)
    ci = lax.broadcasted_iota(jnp.int32, (c, c), 1)
    causal = ri >= ci
    tril = jnp.where(causal, 1.0, 0.0).astype(bf16)
    head0 = lax.broadcasted_iota(jnp.int32, (c, LANES), 1) < GLA_DK
    lane_sq = lax.broadcasted_iota(jnp.int32, (LANES, LANES), 1) < GLA_DK
    block_rows = lambda ib: slice(ib * c, (ib + 1) * c)
    head_cols = lambda h: slice(h * GLA_DV, (h + 1) * GLA_DV)
    head_mask = lambda j: head0 if j == 0 else jnp.logical_not(head0)

    worst = None
    for ib in range(nblock):
        b = _cumsum_rows(g_ref[block_rows(ib), :], tril)
        b_ref[block_rows(ib), :] = b
        b_mid = b[c // 2 - 1:c // 2, :]
        w = jnp.maximum(-b_mid, b_mid - b[c - 1:c, :])
        worst = w if worst is None else jnp.maximum(worst, w)
    safe = jnp.max(worst) < GLA_SAFE_LOG_DECAY

    @pl.when(safe)
    def _():
        for ib in range(nblock):
            rows = block_rows(ib)
            b = b_ref[rows, :]
            b_mid = b[c // 2 - 1:c // 2, :]
            qe = q_ref[rows, :] * jnp.exp(b - b_mid)
            ke = (k_ref[rows, :] * jnp.exp(b_mid - b)).astype(bf16)
            for h in range(GLA_HEADS):
                sl = slice((h // 2) * LANES, (h // 2 + 1) * LANES)
                qe_h = jnp.where(head_mask(h % 2), qe[:, sl], 0.0).astype(bf16)
                a = lax.dot_general(qe_h, ke[:, sl], (((1,), (1,)), ((), ())), preferred_element_type=f32)
                a = jnp.where(causal, a, 0.0).astype(bf16)
                o_ref[rows, head_cols(h)] = jnp.dot(a, v_ref[rows, head_cols(h)], preferred_element_type=f32)

    @pl.when(jnp.logical_not(safe))
    def _():
        tok = lax.broadcasted_iota(jnp.int32, (c, 1), 0)

        def exact_block(ib, carry):
            rows = pl.ds(pl.multiple_of(ib * c, c), c)
            b = b_ref[rows, :]
            q = q_ref[rows, :]
            k = k_ref[rows, :]
            v = v_ref[rows, :].astype(f32)
            pick = lambda x, s: jnp.sum(jnp.where(tok == s, x, 0.0), axis=0, keepdims=True)

            def key(s, acc):
                w = q * pick(k, s) * jnp.exp(jnp.minimum(b - pick(b, s), 0.0))
                w = jnp.where(tok >= s, w, 0.0)
                vs = pick(v, s)
                return tuple(
                    acc[h] + jnp.sum(w[:, h * GLA_DK:(h + 1) * GLA_DK], axis=-1, keepdims=True) * vs[:, head_cols(h)]
                    for h in range(GLA_HEADS))

            acc = lax.fori_loop(0, c, key, tuple(jnp.zeros((c, GLA_DV), f32) for _ in range(GLA_HEADS)))
            for h in range(GLA_HEADS):
                o_ref[rows, head_cols(h)] = acc[h]
            return carry

        lax.fori_loop(0, nblock, exact_block, 0)

    for ib in range(nblock):
        rows = block_rows(ib)
        b = b_ref[rows, :]
        b_last = b[c - 1:c, :]
        qb = q_ref[rows, :] * jnp.exp(b)
        kd = (k_ref[rows, :] * jnp.exp(b_last - b)).astype(bf16)
        e_last = jnp.exp(b_last)
        for p in range(GLA_HEADS // 2):
            sl = slice(p * LANES, (p + 1) * LANES)
            st = st_ref[p]
            stb = st.astype(bf16)
            upd = []
            for j in range(2):
                h = 2 * p + j
                qb_h = jnp.where(head_mask(j), qb[:, sl], 0.0).astype(bf16)
                o_ref[rows, head_cols(h)] += lax.dot_general(qb_h, stb, (((1,), (1,)), ((), ())),
                                                             preferred_element_type=f32)
                upd.append(lax.dot_general(v_ref[rows, head_cols(h)], kd[:, sl], (((0,), (0,)), ((), ())),
                                           preferred_element_type=f32))
            st_ref[p] = st * e_last[:, sl] + jnp.where(lane_sq, upd[0], upd[1])

    @pl.when(step == pl.num_programs(1) - 1)
    def _():
        for p in range(GLA_HEADS // 2):
            t = st_ref[p].T
            s_ref[0, 2 * p] = t[:GLA_DK, :]
            s_ref[0, 2 * p + 1] = t[GLA_DK:, :]


def _gla_prompt(gq, gk, gv, gl, *, batch, seq, tg):
    ns = seq // tg
    row = lambda w: pl.BlockSpec((tg, w), lambda b, i: (b * ns + i, 0))
    return pl.pallas_call(
        functools.partial(_gla_kernel, tg=tg),
        grid=(batch, ns),
        in_specs=[row(GK_W), row(GK_W), row(GV_W), row(GK_W)],
        out_specs=(row(GV_W), pl.BlockSpec((1, GLA_HEADS, GLA_DK, GLA_DV), lambda b, i: (b, 0, 0, 0))),
        out_shape=(jax.ShapeDtypeStruct((batch * seq, GV_W), f32),
                   jax.ShapeDtypeStruct((batch, GLA_HEADS, GLA_DK, GLA_DV), f32)),
        scratch_shapes=[pltpu.VMEM((GLA_HEADS // 2, LANES, LANES), f32), pltpu.VMEM((tg, GK_W), f32)],
        compiler_params=_cparams(("parallel", "arbitrary")),
        name="gla_prompt",
    )(gq, gk, gv, gl)


def _merge_ffn_pieces(x_ref, od_ref, og_ref, gg_ref, sub_ref, gn_ref, wo_ref, n2_ref, lam_init):
    od = od_ref[...]
    odn = od * _group_scale(od, DIFF_DV) * sub_ref[...] * (1.0 - lam_init)
    og = og_ref[...]
    ogn = og * _group_scale(og, GLA_DV) * gn_ref[...]
    gg = gg_ref[...]
    ogn = ogn * (gg * jax.nn.sigmoid(gg))
    mix = jnp.concatenate([odn, ogn], axis=-1).astype(bf16)
    y = x_ref[...] + jnp.dot(mix, wo_ref[0], preferred_element_type=f32)
    n2 = (y * lax.rsqrt(jnp.mean(y * y, axis=-1, keepdims=True) + EPS) * n2_ref[...]).astype(bf16)
    return y, n2


def _ffn_chunk(n2, wgu_ref, wd_ref, c0, ff_chunk, d_ff):
    a = jnp.dot(n2, wgu_ref[0, :, c0:c0 + ff_chunk], preferred_element_type=f32)
    b = jnp.dot(n2, wgu_ref[0, :, d_ff + c0:d_ff + c0 + ff_chunk], preferred_element_type=f32)
    hid = (a * jax.nn.sigmoid(a) * b).astype(bf16)
    return jnp.dot(hid, wd_ref[0, c0:c0 + ff_chunk, :], preferred_element_type=f32)


def _post_kernel(x_ref, od_ref, og_ref, gg_ref, sub_ref, gn_ref, wo_ref, n2_ref, wgu_ref, wd_ref, y_ref,
                 *, lam_init, d_ff, ff_chunk):
    acc, n2 = _merge_ffn_pieces(x_ref, od_ref, og_ref, gg_ref, sub_ref, gn_ref, wo_ref, n2_ref, lam_init)
    for c0 in range(0, d_ff, ff_chunk):
        acc = acc + _ffn_chunk(n2, wgu_ref, wd_ref, c0, ff_chunk, d_ff)
    y_ref[...] = acc


def _post(x2d, od, og, gg, sub, gn, w_out, n2, w_gu, w_down, *, layer, tm, lam_init):
    n, d = x2d.shape
    d_ff = w_down.shape[1]
    row = lambda w: pl.BlockSpec((tm, w), lambda i: (i, 0))
    return pl.pallas_call(
        functools.partial(_post_kernel, lam_init=lam_init, d_ff=d_ff, ff_chunk=2 * LANES),
        grid=(n // tm,),
        in_specs=[row(d), row(DV_W), row(GV_W), row(GV_W), _const_spec((1, DV_W)), _const_spec((1, GV_W)),
                  _layer_spec(w_out.shape, layer), _const_spec((1, d)), _layer_spec(w_gu.shape, layer),
                  _layer_spec(w_down.shape, layer)],
        out_specs=row(d),
        out_shape=jax.ShapeDtypeStruct((n, d), f32),
        compiler_params=_cparams(("parallel",)),
        name="post",
    )(x2d, od, og, gg, sub, gn, w_out, n2, w_gu, w_down)


def _sample_probs(q, k_new, kbuf, slot, *, n_pages, page):
    past = n_pages * page
    nrow = 2 * DIFF_HEADS
    qb = q.astype(bf16)
    rid = lax.broadcasted_iota(jnp.int32, (nrow, past), 0)
    s = jnp.zeros((nrow, past), f32)
    for h in range(DIFF_HEADS):
        for mp in range(2):
            kt = jnp.concatenate([kbuf[slot, j, h, mp] for j in range(n_pages)], axis=-1)
            sr = jnp.dot(qb, kt.astype(bf16), preferred_element_type=f32)
            s = jnp.where(rid == 2 * h + mp, sr, s)
    s_new = jnp.sum(q * k_new, axis=-1, keepdims=True)
    m = jnp.maximum(jnp.max(s, axis=-1, keepdims=True), s_new)
    p = jnp.exp2(s - m)
    p_new = jnp.exp2(s_new - m)
    inv_l = 1.0 / (jnp.sum(p, axis=-1, keepdims=True) + p_new)
    return p.astype(bf16), p_new, inv_l


def _sample_values(pb, p_new, inv_l, v_new, lam, vbuf, slot, *, n_pages, page):
    def head_values(h):
        vh = jnp.concatenate([vbuf[slot, j, pl.ds(h, page, stride=DIFF_HEADS), :] for j in range(n_pages)], axis=0)
        return vh.astype(bf16)

    heads = []
    for h0 in range(0, DIFF_HEADS, 2):
        pair = jnp.dot(pb, jnp.concatenate([head_values(h0), head_values(h0 + 1)], axis=-1),
                       preferred_element_type=f32)
        for h in (h0, h0 + 1):
            oh = pair[:, (h - h0) * DIFF_DV:(h - h0 + 1) * DIFF_DV]
            oh = (oh + p_new * v_new[:, h * DIFF_DV:(h + 1) * DIFF_DV]) * inv_l
            heads.append(oh[2 * h:2 * h + 1, :] - lam * oh[2 * h + 1:2 * h + 2, :])
    return jnp.concatenate(heads, axis=-1)


def _page_copies(pt_ref, cache_hbm, buf, sem, which, sample, slot, *, layer, n_pages):
    return [pltpu.make_async_copy(cache_hbm.at[layer, pt_ref[sample * n_pages + j]], buf.at[slot, j],
                                  sem.at[which, slot]) for j in range(n_pages)]


def _post_attn_kernel(pt_ref, x_ref, od_ref, og_ref, gg_ref, sub_ref, gn_ref, wo_ref, n2_ref, wgu_ref, wd_ref,
                      q_ref, kn_ref, vn_ref, lqk_ref, ck_hbm, cv_hbm, y_ref, os_ref, kbuf, vbuf, sem,
                      *, lam_init, d_ff, ff_chunk, layer, n_pages, page, per_step):
    step = pl.program_id(0)
    nsteps = pl.num_programs(0)
    k_copies = functools.partial(_page_copies, pt_ref, ck_hbm, kbuf, sem, 0, layer=layer, n_pages=n_pages)
    v_copies = functools.partial(_page_copies, pt_ref, cv_hbm, vbuf, sem, 1, layer=layer, n_pages=n_pages)

    nb = nsteps * per_step

    def start_ahead(copies, j):
        g2 = step * per_step + j + 2
        if j + 2 < per_step:
            for cp in copies(g2, j % 2):
                cp.start()
        else:
            @pl.when(g2 < nb)
            def _():
                for cp in copies(g2, j % 2):
                    cp.start()

    @pl.when(step == 0)
    def _():
        for g in range(2):
            for cp in k_copies(g, g) + v_copies(g, g):
                cp.start()

    lam = _diff_lambda(lqk_ref[...], lam_init)

    def sample_probs(j):
        g = step * per_step + j
        slot = j % 2
        for cp in k_copies(g, slot) + v_copies(g, slot):
            cp.wait()
        probs = _sample_probs(q_ref[j], kn_ref[j], kbuf, slot, n_pages=n_pages, page=page)
        start_ahead(k_copies, j)
        return probs

    def sample_values(j, probs):
        os_ref[j] = _sample_values(*probs, vn_ref[j], lam, vbuf, j % 2, n_pages=n_pages, page=page)
        start_ahead(v_copies, j)

    acc, n2 = _merge_ffn_pieces(x_ref, od_ref, og_ref, gg_ref, sub_ref, gn_ref, wo_ref, n2_ref, lam_init)
    nchunk = d_ff // ff_chunk
    sample_at = {(j * nchunk) // per_step: j for j in range(per_step)}
    for ic in range(nchunk):
        if ic in sample_at:
            probs = sample_probs(sample_at[ic])
        acc = acc + _ffn_chunk(n2, wgu_ref, wd_ref, ic * ff_chunk, ff_chunk, d_ff)
        if ic in sample_at:
            sample_values(sample_at[ic], probs)
    y_ref[...] = acc


def _post_attn(x2d, od, og, gg, sub, gn, w_out, n2, w_gu, w_down, page_table, q_s, kn_s, vn_s, lqk, cache_kt, cache_vr,
               *, tm, lam_init, layer):
    n, d = x2d.shape
    d_ff = w_down.shape[1]
    nsteps = n // tm
    nb, n_pages = page_table.shape
    per_step = nb // nsteps
    assert per_step * nsteps == nb and per_step % 2 == 0
    page = cache_kt.shape[-1]
    nrow = 2 * DIFF_HEADS
    row = lambda w: pl.BlockSpec((tm, w), lambda i, pt: (i, 0))
    const = lambda shape: pl.BlockSpec(shape, lambda i, pt: (0,) * len(shape), pipeline_mode=pl.Buffered(1))
    per = lambda r, w: pl.BlockSpec((per_step, r, w), lambda i, pt: (i, 0, 0))
    grid_spec = pltpu.PrefetchScalarGridSpec(
        num_scalar_prefetch=1,
        grid=(nsteps,),
        in_specs=[row(d), row(DV_W), row(GV_W), row(GV_W), const((1, DV_W)), const((1, GV_W)),
                  _layer_spec(w_out.shape, layer), const((1, d)), _layer_spec(w_gu.shape, layer),
                  _layer_spec(w_down.shape, layer),
                  per(nrow, DIFF_DK), per(nrow, DIFF_DK), per(1, DV_W), const(lqk.shape),
                  pl.BlockSpec(memory_space=pl.ANY), pl.BlockSpec(memory_space=pl.ANY)],
        out_specs=(row(d), per(1, DV_W)),
        scratch_shapes=[pltpu.VMEM((2, n_pages) + cache_kt.shape[2:], f32),
                        pltpu.VMEM((2, n_pages) + cache_vr.shape[2:], f32),
                        pltpu.SemaphoreType.DMA((2, 2))],
    )
    y, o_s = pl.pallas_call(
        functools.partial(_post_attn_kernel, lam_init=lam_init, d_ff=d_ff, ff_chunk=2 * LANES, layer=layer,
                          n_pages=n_pages, page=page, per_step=per_step),
        grid_spec=grid_spec,
        out_shape=(jax.ShapeDtypeStruct((n, d), f32), jax.ShapeDtypeStruct((nb, 1, DV_W), f32)),
        compiler_params=pltpu.CompilerParams(dimension_semantics=("arbitrary",), vmem_limit_bytes=FUSED_VMEM_LIMIT),
        name="post_attn",
    )(page_table.reshape(-1), x2d, od, og, gg, sub, gn, w_out, n2, w_gu, w_down,
      q_s.reshape(nb, nrow, DIFF_DK), kn_s.reshape(nb, nrow, DIFF_DK), vn_s.reshape(nb, 1, DV_W), lqk,
      cache_kt, cache_vr)
    return y, o_s.reshape(nb, DV_W)


def _gla_sample_kernel(qt_ref, kt_ref, v_ref, et_ref, s0_ref, o_ref, s_ref, *, tb):
    qt = qt_ref[0]
    kt = kt_ref[0]
    et = et_ref[0]
    v = v_ref[...]
    for i in range(tb):
        for h in range(GLA_HEADS):
            ks = slice(h * GLA_DK, (h + 1) * GLA_DK)
            vs = slice(h * GLA_DV, (h + 1) * GLA_DV)
            s_new = s0_ref[0, i, h] * et[ks, i:i + 1] + kt[ks, i:i + 1] * v[i:i + 1, vs]
            s_ref[i, h] = s_new
            o_ref[i:i + 1, vs] = jnp.sum(qt[ks, i:i + 1] * s_new, axis=0, keepdims=True)


def _gla_sample(gqt, gkt, gv, et, s0_all, *, layer):
    nblk, _, tb = gqt.shape
    tr = pl.BlockSpec((1, GK_W, tb), lambda i: (i, 0, 0))
    row = pl.BlockSpec((tb, GV_W), lambda i: (i, 0))
    st_in = pl.BlockSpec((1, tb, GLA_HEADS, GLA_DK, GLA_DV), lambda i: (layer, i, 0, 0, 0))
    st = pl.BlockSpec((tb, GLA_HEADS, GLA_DK, GLA_DV), lambda i: (i, 0, 0, 0))
    return pl.pallas_call(
        functools.partial(_gla_sample_kernel, tb=tb),
        grid=(nblk,),
        in_specs=[tr, tr, row, tr, st_in],
        out_specs=(row, st),
        out_shape=(jax.ShapeDtypeStruct((nblk * tb, GV_W), f32), jax.ShapeDtypeStruct(s0_all.shape[1:], f32)),
        compiler_params=_cparams(("parallel",)),
        name="gla_sample",
    )(gqt, gkt, gv, et, s0_all)


def _rope_tables(pos):
    half = DIFF_DK // 2
    freqs = ROPE_THETA ** (-jnp.arange(half, dtype=f32) / half)
    ang = pos.astype(f32)[:, None] * freqs[None, :]
    cos, sin = jnp.cos(ang), jnp.sin(ang)
    reps = LANES // DIFF_DK
    return jnp.tile(jnp.concatenate([cos, cos], -1), (1, reps)), jnp.tile(jnp.concatenate([-sin, sin], -1), (1, reps))


def kernel(x_prompt, x_sample, cache_k, cache_v, state_gla, page_table, norm1, w_in, q_norm, k_norm, lambda_qk,
           subln, w_a2, b_a, gla_norm, w_out, norm2, w_gu, w_down):
    batch, seq, d = x_prompt.shape
    nb, dec_seq, _ = x_sample.shape
    assert dec_seq == 1
    depth = w_in.shape[0]
    n_pages = page_table.shape[1]
    page = cache_k.shape[2]
    past = n_pages * page
    rank = w_a2.shape[1]

    cos_p, sin_p = _rope_tables(jnp.arange(seq))
    cos_s, sin_s = _rope_tables(past + jnp.arange(dec_seq))
    cos_s = jnp.broadcast_to(cos_s, (nb, LANES))
    sin_s = jnp.broadcast_to(sin_s, (nb, LANES))

    ck = jnp.transpose(cache_k, (0, 1, 3, 4, 5, 2))
    cv = cache_v.reshape(depth, -1, page * DIFF_HEADS, DIFF_DV)

    yp = x_prompt.reshape(batch * seq, d)
    ys = x_sample.reshape(nb, d)
    w_in_b = w_in.astype(bf16)
    w_r = jnp.pad(w_in[:, :, MAIN_W:], ((0, 0), (0, 0), (0, LANES - rank))).astype(bf16)
    wa2 = jnp.pad(w_a2, ((0, 0), (0, LANES - rank), (0, 0))).astype(bf16)
    w_out_b, w_gu_b, w_down_b = w_out.astype(bf16), w_gu.astype(bf16), w_down.astype(bf16)

    sp, k_s, v_s, s_s = [], [], [], []
    kv_prompt = None
    for l in range(depth):
        lam_init = 0.8 - 0.6 * math.exp(-0.3 * l)
        proj_args = (norm1[l][None], w_in_b, w_r, wa2, b_a[l][None],
                     jnp.tile(q_norm[l], QK_W // DIFF_DK)[None], jnp.tile(k_norm[l], QK_W // DIFF_DK)[None])
        post_args = (jnp.tile(subln[l], DIFF_HEADS)[None], jnp.tile(gla_norm[l], GLA_HEADS)[None],
                     w_out_b, norm2[l][None], w_gu_b, w_down_b)

        q, kb, kt, vb, v4, gq, gk, gv, gl, gg = _proj_prompt(yp, *proj_args, cos_p, sin_p, layer=l, batch=batch,
                                                             tm=512, prev_kv=kv_prompt)
        kv_prompt = (kt, v4)
        q_s, k_s_l, v_s_l, gqt, gkt, gv_s, et, gg_s = _proj_sample(ys, *proj_args, cos_s, sin_s, layer=l,
                                                                   tb=GLA_SAMPLE_BLOCK)
        od = _attn_prompt(q, kb, vb, lambda_qk[l], batch=batch, seq=seq, tq=256, heads=2, lam_init=lam_init)
        og, s_fin = _gla_prompt(gq, gk, gv, gl, batch=batch, seq=seq, tg=512)
        yp, od_s = _post_attn(yp, od, og, gg, *post_args, page_table, q_s, k_s_l, v_s_l, lambda_qk[l], ck, cv,
                              tm=512, lam_init=lam_init, layer=l)
        sp.append(s_fin)

        og_s, s_new = _gla_sample(gqt, gkt, gv_s, et, state_gla, layer=l)
        ys = _post(ys, od_s, og_s, gg_s, *post_args, layer=l, tm=nb, lam_init=lam_init)
        k_s.append(k_s_l.reshape(nb, dec_seq, DIFF_HEADS, 2, DIFF_DK))
        v_s.append(v_s_l.reshape(nb, dec_seq, DIFF_HEADS, DIFF_DV))
        s_s.append(s_new)

    k_prompt = kv_prompt[0].reshape(depth, batch, DIFF_HEADS, 2, DIFF_DK, seq).transpose(0, 1, 5, 2, 3, 4)
    v_prompt = kv_prompt[1].reshape(depth, batch, seq, DIFF_HEADS, DIFF_DV)
    return (yp.reshape(batch, seq, d), ys.reshape(nb, dec_seq, d), k_prompt, v_prompt, jnp.stack(sp),
            jnp.stack(k_s), jnp.stack(v_s), jnp.stack(s_s))
```

```python
import functools
import math

import jax
import jax.numpy as jnp
from jax import lax
from jax.experimental import pallas as pl
from jax.experimental.pallas import tpu as pltpu

f32 = jnp.float32
bf16 = jnp.bfloat16

DIFF_HEADS = 4
DIFF_DK = 64
DIFF_DV = 128
GLA_HEADS = 4
GLA_DK = 64
GLA_DV = 128
GLA_GATE_NORM = 16.0
ROPE_THETA = 10000.0
EPS = 1e-6
LOG2_E = math.log2(math.e)

QK_W = DIFF_HEADS * 2 * DIFF_DK
DV_W = DIFF_HEADS * DIFF_DV
GK_W = GLA_HEADS * GLA_DK
GV_W = GLA_HEADS * GLA_DV
MAIN_W = 2 * QK_W + DV_W + 2 * GK_W + 2 * GV_W

LANES = 128
ROW_TILE = 512
ATTN_Q_TILE = 256
ATTN_HEADS_PER_STEP = 2
GLA_STEP_TOKENS = 512
GLA_SAMPLE_BLOCK = 32
GLA_BLOCK = 256
GLA_SAFE_LOG_DECAY = 80.0
VMEM_LIMIT = 56 * 1024 * 1024
FUSED_VMEM_LIMIT = 60 * 1024 * 1024


def _cparams(sem):
    return pltpu.CompilerParams(dimension_semantics=sem, vmem_limit_bytes=VMEM_LIMIT)


def _const_spec(shape):
    nd = len(shape)
    return pl.BlockSpec(shape, lambda *_: (0,) * nd, pipeline_mode=pl.Buffered(1))


def _layer_spec(shape, layer):
    nd = len(shape)
    return pl.BlockSpec((1,) + tuple(shape[1:]), lambda *_: (layer,) + (0,) * (nd - 1), pipeline_mode=pl.Buffered(1))


def _group_scale(z, width):
    lane = lax.broadcasted_iota(jnp.int32, (z.shape[0], LANES), 1)
    cols = []
    for c in range(z.shape[1] // LANES):
        zc = z[:, c * LANES:(c + 1) * LANES]
        zz = zc * zc
        if width == LANES:
            cols.append(jnp.broadcast_to(lax.rsqrt(jnp.mean(zz, axis=-1, keepdims=True) + EPS), zc.shape))
        else:
            lo = lane < width
            s_lo = jnp.sum(jnp.where(lo, zz, 0.0), axis=-1, keepdims=True)
            s_hi = jnp.sum(jnp.where(lo, 0.0, zz), axis=-1, keepdims=True)
            r_lo = lax.rsqrt(s_lo * (1.0 / width) + EPS)
            r_hi = lax.rsqrt(s_hi * (1.0 / width) + EPS)
            cols.append(jnp.where(lo, r_lo, r_hi))
    return jnp.concatenate(cols, axis=-1)


def _rope(z, cos, sin_signed):
    lane = lax.broadcasted_iota(jnp.int32, (z.shape[0], LANES), 1)
    first_half = (lane % DIFF_DK) < (DIFF_DK // 2)
    cols = []
    for c in range(z.shape[1] // LANES):
        zc = z[:, c * LANES:(c + 1) * LANES]
        partner = jnp.where(first_half, pltpu.roll(zc, LANES - DIFF_DK // 2, 1), pltpu.roll(zc, DIFF_DK // 2, 1))
        cols.append(zc * cos + partner * sin_signed)
    return jnp.concatenate(cols, axis=-1)


def _diff_lambda(lqk, lam_init):
    a = jnp.sum(lqk[0:1, :] * lqk[1:2, :], axis=-1, keepdims=True)
    b = jnp.sum(lqk[2:3, :] * lqk[3:4, :], axis=-1, keepdims=True)
    return jnp.exp(a) - jnp.exp(b) + lam_init


def _proj_body(x, g1_ref, w_ref, wr_ref, wa2_ref, ba_ref, qn_ref, kn_ref, cos, sin):
    n = x * lax.rsqrt(jnp.mean(x * x, axis=-1, keepdims=True) + EPS) * g1_ref[...]
    nb = n.astype(bf16)

    def seg(lo, width):
        return jnp.dot(nb, w_ref[0, :, lo:lo + width], preferred_element_type=f32)

    zq = seg(0, QK_W)
    q = _rope(zq * _group_scale(zq, DIFF_DK) * qn_ref[...], cos, sin) * (DIFF_DK ** -0.5 * LOG2_E)
    zk = seg(QK_W, QK_W)
    k = _rope(zk * _group_scale(zk, DIFF_DK) * kn_ref[...], cos, sin)
    v = seg(2 * QK_W, DV_W)
    off = 2 * QK_W + DV_W
    gq = seg(off, GK_W) * (GLA_DK ** -0.5)
    gk = seg(off + GK_W, GK_W)
    gv = seg(off + 2 * GK_W, GV_W)
    gg = seg(off + 2 * GK_W + GV_W, GV_W)
    r = jnp.dot(nb, wr_ref[0], preferred_element_type=f32)
    a = jnp.dot(r.astype(bf16), wa2_ref[0], preferred_element_type=f32) + ba_ref[...]
    gl = (jnp.minimum(a, 0.0) - jnp.log1p(jnp.exp(-jnp.abs(a)))) * (1.0 / GLA_GATE_NORM)
    return q, k, v, gq, gk, gv, gl, gg


def _proj_prompt_kernel(x_ref, g1_ref, w_ref, wr_ref, wa2_ref, ba_ref, qn_ref, kn_ref, cos_ref, sin_ref, *rest):
    q_ref, kb_ref, kt_ref, vb_ref, v4_ref, gq_ref, gk_ref, gv_ref, gl_ref, gg_ref = rest[-10:]
    layer = kt_ref.shape[0] - 1
    if layer:
        prev_kt_ref, prev_v4_ref = rest[:2]
        kt_ref[:layer] = prev_kt_ref[...]
        v4_ref[:layer] = prev_v4_ref[...]
    tm = x_ref.shape[0]
    q, k, v, gq, gk, gv, gl, gg = _proj_body(x_ref[...], g1_ref, w_ref, wr_ref, wa2_ref, ba_ref, qn_ref, kn_ref,
                                             cos_ref[...], sin_ref[...])
    first_map = lax.broadcasted_iota(jnp.int32, (tm, LANES), 1) < DIFF_DK
    for h in range(DIFF_HEADS):
        qh = q[:, h * LANES:(h + 1) * LANES]
        q_ref[:, (2 * h) * LANES:(2 * h + 1) * LANES] = jnp.where(first_map, qh, 0.0).astype(bf16)
        q_ref[:, (2 * h + 1) * LANES:(2 * h + 2) * LANES] = jnp.where(first_map, 0.0, qh).astype(bf16)
    kb_ref[...] = k.astype(bf16)
    kt_ref[layer, 0] = k.T
    vb_ref[...] = v.astype(bf16)
    for h in range(DIFF_HEADS):
        v4_ref[layer, pl.ds(h, tm, stride=DIFF_HEADS), :] = v[:, h * DIFF_DV:(h + 1) * DIFF_DV]
    gq_ref[...] = gq
    gk_ref[...] = gk
    gv_ref[...] = gv.astype(bf16)
    gl_ref[...] = gl
    gg_ref[...] = gg


def _proj_prompt(x2d, g1, w_main, w_r, w_a2, b_a, qn, kn, cos_t, sin_t, *, layer, batch, tm, prev_kv):
    n, d = x2d.shape
    seq = n // batch
    nt = seq // tm
    row = lambda w: pl.BlockSpec((tm, w), lambda b, j: (b * nt + j, 0))
    tab = pl.BlockSpec((tm, LANES), lambda b, j: (j, 0))
    arr = lambda w, dt: jax.ShapeDtypeStruct((n, w), dt)
    kt_spec = lambda nl: pl.BlockSpec((nl, 1, QK_W, tm), lambda b, j: (0, b, 0, j))
    v4_spec = lambda nl: pl.BlockSpec((nl, tm * DIFF_HEADS, DIFF_DV), lambda b, j: (0, b * nt + j, 0))
    out_shape = (
        arr(2 * QK_W, bf16),
        arr(QK_W, bf16),
        jax.ShapeDtypeStruct((layer + 1, batch, QK_W, seq), f32),
        arr(DV_W, bf16),
        jax.ShapeDtypeStruct((layer + 1, n * DIFF_HEADS, DIFF_DV), f32),
        arr(GK_W, f32), arr(GK_W, f32), arr(GV_W, bf16), arr(GK_W, f32), arr(GV_W, f32),
    )
    out_specs = (row(2 * QK_W), row(QK_W), kt_spec(layer + 1), row(DV_W), v4_spec(layer + 1),
                 row(GK_W), row(GK_W), row(GV_W), row(GK_W), row(GV_W))
    prev = () if prev_kv is None else tuple(prev_kv)
    return pl.pallas_call(
        _proj_prompt_kernel,
        grid=(batch, nt),
        in_specs=[row(d), _const_spec((1, d)), _layer_spec(w_main.shape, layer), _layer_spec(w_r.shape, layer),
                  _layer_spec(w_a2.shape, layer), _const_spec((1, GK_W)), _const_spec((1, QK_W)),
                  _const_spec((1, QK_W)), tab, tab] + ([kt_spec(layer), v4_spec(layer)] if prev else []),
        out_specs=out_specs,
        out_shape=out_shape,
        compiler_params=_cparams(("parallel", "parallel")),
        name="proj_prompt",
    )(x2d, g1, w_main, w_r, w_a2, b_a, qn, kn, cos_t, sin_t, *prev)


def _proj_sample_kernel(x_ref, g1_ref, w_ref, wr_ref, wa2_ref, ba_ref, qn_ref, kn_ref, cos_ref, sin_ref,
                        q_ref, k_ref, v_ref, gqt_ref, gkt_ref, gv_ref, et_ref, gg_ref):
    q, k, v, gq, gk, gv, gl, gg = _proj_body(x_ref[...], g1_ref, w_ref, wr_ref, wa2_ref, ba_ref, qn_ref, kn_ref,
                                             cos_ref[...], sin_ref[...])
    q_ref[...] = q
    k_ref[...] = k
    v_ref[...] = v
    gv_ref[...] = gv
    gg_ref[...] = gg
    e = jnp.exp(gl)
    tb = gqt_ref.shape[2]
    for blk in range(gqt_ref.shape[0]):
        rows = slice(blk * tb, (blk + 1) * tb)
        gqt_ref[blk] = gq[rows, :].T
        gkt_ref[blk] = gk[rows, :].T
        et_ref[blk] = e[rows, :].T


def _proj_sample(x2d, g1, w_main, w_r, w_a2, b_a, qn, kn, cos_t, sin_t, *, layer, tb):
    n, d = x2d.shape
    flat = lambda w: jax.ShapeDtypeStruct((n, w), f32)
    tr = jax.ShapeDtypeStruct((n // tb, GK_W, tb), f32)
    out_shape = (flat(QK_W), flat(QK_W), flat(DV_W), tr, tr, flat(GV_W), tr, flat(GV_W))
    whole = lambda shape: pl.BlockSpec(shape, lambda i: (0,) * len(shape))
    return pl.pallas_call(
        _proj_sample_kernel,
        grid=(1,),
        in_specs=[whole(x2d.shape), whole((1, d)), _layer_spec(w_main.shape, layer), _layer_spec(w_r.shape, layer),
                  _layer_spec(w_a2.shape, layer), whole((1, GK_W)), whole((1, QK_W)), whole((1, QK_W)),
                  whole(cos_t.shape), whole(sin_t.shape)],
        out_specs=tuple(whole(o.shape) for o in out_shape),
        out_shape=out_shape,
        compiler_params=_cparams(("arbitrary",)),
        name="proj_sample",
    )(x2d, g1, w_main, w_r, w_a2, b_a, qn, kn, cos_t, sin_t)


def _attn_kernel(q_ref, kb_ref, v_ref, lqk_ref, o_ref, vb_ref, s_ref, *, tq, heads, lam_init):
    seq = kb_ref.shape[0]
    lam = _diff_lambda(lqk_ref[...], lam_init)
    row = lax.broadcasted_iota(jnp.int32, (2 * tq, tq), 0) % tq
    col = lax.broadcasted_iota(jnp.int32, (2 * tq, tq), 1)
    visible = col <= row
    nq = seq // tq
    for hh in range(heads):
        kcols = slice(hh * LANES, (hh + 1) * LANES)
        vb_ref[hh, :, :DIFF_DV] = v_ref[:, kcols]
        vb_ref[hh, :, DIFF_DV:] = jnp.ones((seq, LANES), bf16)
        for qi in (range(nq) if hh % 2 == 0 else reversed(range(nq))):
            rows = slice(qi * tq, (qi + 1) * tq)
            qs = jnp.concatenate([q_ref[rows, (2 * hh) * LANES:(2 * hh + 1) * LANES],
                                  q_ref[rows, (2 * hh + 1) * LANES:(2 * hh + 2) * LANES]], axis=0)
            mx = None
            for kt in range(qi + 1):
                cols = slice(kt * tq, (kt + 1) * tq)
                s = lax.dot_general(qs, kb_ref[cols, kcols], (((1,), (1,)), ((), ())), preferred_element_type=f32)
                if kt == qi:
                    s = jnp.where(visible, s, -jnp.inf)
                s_ref[hh, :, cols] = s
                for c in range(tq // LANES):
                    sc = s[:, c * LANES:(c + 1) * LANES]
                    mx = sc if mx is None else jnp.maximum(mx, sc)
            m = jnp.broadcast_to(jnp.max(mx, axis=-1, keepdims=True), (2 * tq, tq))
            acc = None
            for kt in range(qi + 1):
                cols = slice(kt * tq, (kt + 1) * tq)
                p = jnp.exp2(s_ref[hh, :, cols] - m).astype(bf16)
                pv = jnp.dot(p, vb_ref[hh, cols, :], preferred_element_type=f32)
                acc = pv if acc is None else acc + pv
            o1 = acc[:tq, :DIFF_DV] / acc[:tq, DIFF_DV:]
            o2 = acc[tq:, :DIFF_DV] / acc[tq:, DIFF_DV:]
            o_ref[rows, hh * DIFF_DV:(hh + 1) * DIFF_DV] = o1 - lam * o2


def _attn_prompt(q, kb, vb, lqk, *, batch, seq, tq, heads, lam_init):
    return pl.pallas_call(
        functools.partial(_attn_kernel, tq=tq, heads=heads, lam_init=lam_init),
        grid=(batch, DIFF_HEADS // heads),
        in_specs=[pl.BlockSpec((seq, heads * 2 * LANES), lambda b, g: (b, g)),
                  pl.BlockSpec((seq, heads * LANES), lambda b, g: (b, g)),
                  pl.BlockSpec((seq, heads * LANES), lambda b, g: (b, g)),
                  _const_spec(lqk.shape)],
        out_specs=pl.BlockSpec((seq, heads * DIFF_DV), lambda b, g: (b, g)),
        out_shape=jax.ShapeDtypeStruct((batch * seq, DV_W), f32),
        scratch_shapes=[pltpu.VMEM((heads, seq, 2 * LANES), bf16), pltpu.VMEM((heads, 2 * tq, seq), f32)],
        compiler_params=_cparams(("parallel", "parallel")),
        name="attn_prompt",
    )(q, kb, vb, lqk)


def _cumsum_rows(g, tril):
    g1 = g.astype(bf16)
    r1 = g - g1.astype(f32)
    g2 = r1.astype(bf16)
    g3 = (r1 - g2.astype(f32)).astype(bf16)
    dot = lambda t: jnp.dot(tril, t, preferred_element_type=f32)
    return dot(g1) + dot(g2) + dot(g3)


def _gla_kernel(q_ref, k_ref, v_ref, g_ref, o_ref, s_ref, st_ref, b_ref, *, tg):
    step = pl.program_id(1)

    @pl.when(step == 0)
    def _():
        st_ref[...] = jnp.zeros(st_ref.shape, f32)

    c = GLA_BLOCK
    nblock = tg // c
    ri = lax.broadcasted_iota(jnp.int32, (c, c), 0)
    ci = lax.broadcasted_iota(jnp.int32, (c, c), 1)
    causal = ri >= ci
    tril = jnp.where(causal, 1.0, 0.0).astype(bf16)
    head0 = lax.broadcasted_iota(jnp.int32, (c, LANES), 1) < GLA_DK
    lane_sq = lax.broadcasted_iota(jnp.int32, (LANES, LANES), 1) < GLA_DK
    block_rows = lambda ib: slice(ib * c, (ib + 1) * c)
    head_cols = lambda h: slice(h * GLA_DV, (h + 1) * GLA_DV)
    head_mask = lambda j: head0 if j == 0 else jnp.logical_not(head0)

    worst = None
    for ib in range(nblock):
        b = _cumsum_rows(g_ref[block_rows(ib), :], tril)
        b_ref[block_rows(ib), :] = b
        b_mid = b[c // 2 - 1:c // 2, :]
        w = jnp.maximum(-b_mid, b_mid - b[c - 1:c, :])
        worst = w if worst is None else jnp.maximum(worst, w)
    safe = jnp.max(worst) < GLA_SAFE_LOG_DECAY

    @pl.when(safe)
    def _():
        for ib in range(nblock):
            rows = block_rows(ib)
            b = b_ref[rows, :]
            b_mid = b[c // 2 - 1:c // 2, :]
            qe = q_ref[rows, :] * jnp.exp(b - b_mid)
            ke = (k_ref[rows, :] * jnp.exp(b_mid - b)).astype(bf16)
            for h in range(GLA_HEADS):
                sl = slice((h // 2) * LANES, (h // 2 + 1) * LANES)
                qe_h = jnp.where(head_mask(h % 2), qe[:, sl], 0.0).astype(bf16)
                a = lax.dot_general(qe_h, ke[:, sl], (((1,), (1,)), ((), ())), preferred_element_type=f32)
                a = jnp.where(causal, a, 0.0).astype(bf16)
                o_ref[rows, head_cols(h)] = jnp.dot(a, v_ref[rows, head_cols(h)], preferred_element_type=f32)

    @pl.when(jnp.logical_not(safe))
    def _():
        tok = lax.broadcasted_iota(jnp.int32, (c, 1), 0)

        def exact_block(ib, carry):
            rows = pl.ds(pl.multiple_of(ib * c, c), c)
            b = b_ref[rows, :]
            q = q_ref[rows, :]
            k = k_ref[rows, :]
            v = v_ref[rows, :].astype(f32)
            pick = lambda x, s: jnp.sum(jnp.where(tok == s, x, 0.0), axis=0, keepdims=True)

            def key(s, acc):
                w = q * pick(k, s) * jnp.exp(jnp.minimum(b - pick(b, s), 0.0))
                w = jnp.where(tok >= s, w, 0.0)
                vs = pick(v, s)
                return tuple(
                    acc[h] + jnp.sum(w[:, h * GLA_DK:(h + 1) * GLA_DK], axis=-1, keepdims=True) * vs[:, head_cols(h)]
                    for h in range(GLA_HEADS))

            acc = lax.fori_loop(0, c, key, tuple(jnp.zeros((c, GLA_DV), f32) for _ in range(GLA_HEADS)))
            for h in range(GLA_HEADS):
                o_ref[rows, head_cols(h)] = acc[h]
            return carry

        lax.fori_loop(0, nblock, exact_block, 0)

    for ib in range(nblock):
        rows = block_rows(ib)
        b = b_ref[rows, :]
        b_last = b[c - 1:c, :]
        qb = q_ref[rows, :] * jnp.exp(b)
        kd = (k_ref[rows, :] * jnp.exp(b_last - b)).astype(bf16)
        e_last = jnp.exp(b_last)
        for p in range(GLA_HEADS // 2):
            sl = slice(p * LANES, (p + 1) * LANES)
            st = st_ref[p]
            stb = st.astype(bf16)
            upd = []
            for j in range(2):
                h = 2 * p + j
                qb_h = jnp.where(head_mask(j), qb[:, sl], 0.0).astype(bf16)
                o_ref[rows, head_cols(h)] += lax.dot_general(qb_h, stb, (((1,), (1,)), ((), ())),
                                                             preferred_element_type=f32)
                upd.append(lax.dot_general(v_ref[rows, head_cols(h)], kd[:, sl], (((0,), (0,)), ((), ())),
                                           preferred_element_type=f32))
            st_ref[p] = st * e_last[:, sl] + jnp.where(lane_sq, upd[0], upd[1])

    @pl.when(step == pl.num_programs(1) - 1)
    def _():
        for p in range(GLA_HEADS // 2):
            t = st_ref[p].T
            s_ref[0, 2 * p] = t[:GLA_DK, :]
            s_ref[0, 2 * p + 1] = t[GLA_DK:, :]


def _gla_prompt(gq, gk, gv, gl, *, batch, seq, tg):
    ns = seq // tg
    row = lambda w: pl.BlockSpec((tg, w), lambda b, i: (b * ns + i, 0))
    return pl.pallas_call(
        functools.partial(_gla_kernel, tg=tg),
        grid=(batch, ns),
        in_specs=[row(GK_W), row(GK_W), row(GV_W), row(GK_W)],
        out_specs=(row(GV_W), pl.BlockSpec((1, GLA_HEADS, GLA_DK, GLA_DV), lambda b, i: (b, 0, 0, 0))),
        out_shape=(jax.ShapeDtypeStruct((batch * seq, GV_W), f32),
                   jax.ShapeDtypeStruct((batch, GLA_HEADS, GLA_DK, GLA_DV), f32)),
        scratch_shapes=[pltpu.VMEM((GLA_HEADS // 2, LANES, LANES), f32), pltpu.VMEM((tg, GK_W), f32)],
        compiler_params=_cparams(("parallel", "arbitrary")),
        name="gla_prompt",
    )(gq, gk, gv, gl)


def _merge_ffn_pieces(x_ref, od_ref, og_ref, gg_ref, sub_ref, gn_ref, wo_ref, n2_ref, lam_init):
    od = od_ref[...]
    odn = od * _group_scale(od, DIFF_DV) * sub_ref[...] * (1.0 - lam_init)
    og = og_ref[...]
    ogn = og * _group_scale(og, GLA_DV) * gn_ref[...]
    gg = gg_ref[...]
    ogn = ogn * (gg * jax.nn.sigmoid(gg))
    mix = jnp.concatenate([odn, ogn], axis=-1).astype(bf16)
    y = x_ref[...] + jnp.dot(mix, wo_ref[0], preferred_element_type=f32)
    n2 = (y * lax.rsqrt(jnp.mean(y * y, axis=-1, keepdims=True) + EPS) * n2_ref[...]).astype(bf16)
    return y, n2


def _ffn_chunk(n2, wgu_ref, wd_ref, c0, ff_chunk, d_ff):
    a = jnp.dot(n2, wgu_ref[0, :, c0:c0 + ff_chunk], preferred_element_type=f32)
    b = jnp.dot(n2, wgu_ref[0, :, d_ff + c0:d_ff + c0 + ff_chunk], preferred_element_type=f32)
    hid = (a * jax.nn.sigmoid(a) * b).astype(bf16)
    return jnp.dot(hid, wd_ref[0, c0:c0 + ff_chunk, :], preferred_element_type=f32)


def _post_kernel(x_ref, od_ref, og_ref, gg_ref, sub_ref, gn_ref, wo_ref, n2_ref, wgu_ref, wd_ref, y_ref,
                 *, lam_init, d_ff, ff_chunk):
    acc, n2 = _merge_ffn_pieces(x_ref, od_ref, og_ref, gg_ref, sub_ref, gn_ref, wo_ref, n2_ref, lam_init)
    for c0 in range(0, d_ff, ff_chunk):
        acc = acc + _ffn_chunk(n2, wgu_ref, wd_ref, c0, ff_chunk, d_ff)
    y_ref[...] = acc


def _post(x2d, od, og, gg, sub, gn, w_out, n2, w_gu, w_down, *, layer, tm, lam_init):
    n, d = x2d.shape
    d_ff = w_down.shape[1]
    row = lambda w: pl.BlockSpec((tm, w), lambda i: (i, 0))
    return pl.pallas_call(
        functools.partial(_post_kernel, lam_init=lam_init, d_ff=d_ff, ff_chunk=2 * LANES),
        grid=(n // tm,),
        in_specs=[row(d), row(DV_W), row(GV_W), row(GV_W), _const_spec((1, DV_W)), _const_spec((1, GV_W)),
                  _layer_spec(w_out.shape, layer), _const_spec((1, d)), _layer_spec(w_gu.shape, layer),
                  _layer_spec(w_down.shape, layer)],
        out_specs=row(d),
        out_shape=jax.ShapeDtypeStruct((n, d), f32),
        compiler_params=_cparams(("parallel",)),
        name="post",
    )(x2d, od, og, gg, sub, gn, w_out, n2, w_gu, w_down)


def _sample_probs(q, k_new, kbuf, slot, *, n_pages, page):
    past = n_pages * page
    nrow = 2 * DIFF_HEADS
    qb = q.astype(bf16)
    rid = lax.broadcasted_iota(jnp.int32, (nrow, past), 0)
    s = jnp.zeros((nrow, past), f32)
    for h in range(DIFF_HEADS):
        for mp in range(2):
            kt = jnp.concatenate([kbuf[slot, j, h, mp] for j in range(n_pages)], axis=-1)
            sr = jnp.dot(qb, kt.astype(bf16), preferred_element_type=f32)
            s = jnp.where(rid == 2 * h + mp, sr, s)
    s_new = jnp.sum(q * k_new, axis=-1, keepdims=True)
    m = jnp.maximum(jnp.max(s, axis=-1, keepdims=True), s_new)
    p = jnp.exp2(s - m)
    p_new = jnp.exp2(s_new - m)
    inv_l = 1.0 / (jnp.sum(p, axis=-1, keepdims=True) + p_new)
    return p.astype(bf16), p_new, inv_l


def _sample_values(pb, p_new, inv_l, v_new, lam, vbuf, slot, *, n_pages, page):
    def head_values(h):
        vh = jnp.concatenate([vbuf[slot, j, pl.ds(h, page, stride=DIFF_HEADS), :] for j in range(n_pages)], axis=0)
        return vh.astype(bf16)

    heads = []
    for h0 in range(0, DIFF_HEADS, 2):
        pair = jnp.dot(pb, jnp.concatenate([head_values(h0), head_values(h0 + 1)], axis=-1),
                       preferred_element_type=f32)
        for h in (h0, h0 + 1):
            oh = pair[:, (h - h0) * DIFF_DV:(h - h0 + 1) * DIFF_DV]
            oh = (oh + p_new * v_new[:, h * DIFF_DV:(h + 1) * DIFF_DV]) * inv_l
            heads.append(oh[2 * h:2 * h + 1, :] - lam * oh[2 * h + 1:2 * h + 2, :])
    return jnp.concatenate(heads, axis=-1)


def _page_copies(pt_ref, cache_hbm, buf, sem, which, sample, slot, *, layer, n_pages):
    return [pltpu.make_async_copy(cache_hbm.at[layer, pt_ref[sample * n_pages + j]], buf.at[slot, j],
                                  sem.at[which, slot]) for j in range(n_pages)]


def _post_attn_kernel(pt_ref, x_ref, od_ref, og_ref, gg_ref, sub_ref, gn_ref, wo_ref, n2_ref, wgu_ref, wd_ref,
                      q_ref, kn_ref, vn_ref, lqk_ref, ck_hbm, cv_hbm, y_ref, os_ref, kbuf, vbuf, sem,
                      *, lam_init, d_ff, ff_chunk, layer, n_pages, page, per_step):
    step = pl.program_id(0)
    nsteps = pl.num_programs(0)
    k_copies = functools.partial(_page_copies, pt_ref, ck_hbm, kbuf, sem, 0, layer=layer, n_pages=n_pages)
    v_copies = functools.partial(_page_copies, pt_ref, cv_hbm, vbuf, sem, 1, layer=layer, n_pages=n_pages)

    nb = nsteps * per_step

    def start_ahead(copies, j):
        g2 = step * per_step + j + 2
        if j + 2 < per_step:
            for cp in copies(g2, j % 2):
                cp.start()
        else:
            @pl.when(g2 < nb)
            def _():
                for cp in copies(g2, j % 2):
                    cp.start()

    @pl.when(step == 0)
    def _():
        for g in range(2):
            for cp in k_copies(g, g) + v_copies(g, g):
                cp.start()

    lam = _diff_lambda(lqk_ref[...], lam_init)

    def sample_probs(j):
        g = step * per_step + j
        slot = j % 2
        for cp in k_copies(g, slot) + v_copies(g, slot):
            cp.wait()
        probs = _sample_probs(q_ref[j], kn_ref[j], kbuf, slot, n_pages=n_pages, page=page)
        start_ahead(k_copies, j)
        return probs

    def sample_values(j, probs):
        os_ref[j] = _sample_values(*probs, vn_ref[j], lam, vbuf, j % 2, n_pages=n_pages, page=page)
        start_ahead(v_copies, j)

    acc, n2 = _merge_ffn_pieces(x_ref, od_ref, og_ref, gg_ref, sub_ref, gn_ref, wo_ref, n2_ref, lam_init)
    nchunk = d_ff // ff_chunk
    sample_at = {(j * nchunk) // per_step: j for j in range(per_step)}
    for ic in range(nchunk):
        if ic in sample_at:
            probs = sample_probs(sample_at[ic])
        acc = acc + _ffn_chunk(n2, wgu_ref, wd_ref, ic * ff_chunk, ff_chunk, d_ff)
        if ic in sample_at:
            sample_values(sample_at[ic], probs)
    y_ref[...] = acc


def _post_attn(x2d, od, og, gg, sub, gn, w_out, n2, w_gu, w_down, page_table, q_s, kn_s, vn_s, lqk, cache_kt, cache_vr,
               *, tm, lam_init, layer):
    n, d = x2d.shape
    d_ff = w_down.shape[1]
    nsteps = n // tm
    nb, n_pages = page_table.shape
    per_step = nb // nsteps
    assert per_step * nsteps == nb and per_step % 2 == 0
    page = cache_kt.shape[-1]
    nrow = 2 * DIFF_HEADS
    row = lambda w: pl.BlockSpec((tm, w), lambda i, pt: (i, 0))
    const = lambda shape: pl.BlockSpec(shape, lambda i, pt: (0,) * len(shape), pipeline_mode=pl.Buffered(1))
    per = lambda r, w: pl.BlockSpec((per_step, r, w), lambda i, pt: (i, 0, 0))
    grid_spec = pltpu.PrefetchScalarGridSpec(
        num_scalar_prefetch=1,
        grid=(nsteps,),
        in_specs=[row(d), row(DV_W), row(GV_W), row(GV_W), const((1, DV_W)), const((1, GV_W)),
                  _layer_spec(w_out.shape, layer), const((1, d)), _layer_spec(w_gu.shape, layer),
                  _layer_spec(w_down.shape, layer),
                  per(nrow, DIFF_DK), per(nrow, DIFF_DK), per(1, DV_W), const(lqk.shape),
                  pl.BlockSpec(memory_space=pl.ANY), pl.BlockSpec(memory_space=pl.ANY)],
        out_specs=(row(d), per(1, DV_W)),
        scratch_shapes=[pltpu.VMEM((2, n_pages) + cache_kt.shape[2:], f32),
                        pltpu.VMEM((2, n_pages) + cache_vr.shape[2:], f32),
                        pltpu.SemaphoreType.DMA((2, 2))],
    )
    y, o_s = pl.pallas_call(
        functools.partial(_post_attn_kernel, lam_init=lam_init, d_ff=d_ff, ff_chunk=2 * LANES, layer=layer,
                          n_pages=n_pages, page=page, per_step=per_step),
        grid_spec=grid_spec,
        out_shape=(jax.ShapeDtypeStruct((n, d), f32), jax.ShapeDtypeStruct((nb, 1, DV_W), f32)),
        compiler_params=pltpu.CompilerParams(dimension_semantics=("arbitrary",), vmem_limit_bytes=FUSED_VMEM_LIMIT),
        name="post_attn",
    )(page_table.reshape(-1), x2d, od, og, gg, sub, gn, w_out, n2, w_gu, w_down,
      q_s.reshape(nb, nrow, DIFF_DK), kn_s.reshape(nb, nrow, DIFF_DK), vn_s.reshape(nb, 1, DV_W), lqk,
      cache_kt, cache_vr)
    return y, o_s.reshape(nb, DV_W)


def _gla_sample_kernel(qt_ref, kt_ref, v_ref, et_ref, s0_ref, *rest, tb):
    o_ref, s_ref = rest[-2:]
    layer = s_ref.shape[0] - 1
    if layer:
        s_ref[:layer] = rest[0][...]
    qt = qt_ref[0]
    kt = kt_ref[0]
    et = et_ref[0]
    v = v_ref[...]
    for i in range(tb):
        for h in range(GLA_HEADS):
            ks = slice(h * GLA_DK, (h + 1) * GLA_DK)
            vs = slice(h * GLA_DV, (h + 1) * GLA_DV)
            s_new = s0_ref[0, i, h] * et[ks, i:i + 1] + kt[ks, i:i + 1] * v[i:i + 1, vs]
            s_ref[layer, i, h] = s_new
            o_ref[i:i + 1, vs] = jnp.sum(qt[ks, i:i + 1] * s_new, axis=0, keepdims=True)


def _gla_sample(gqt, gkt, gv, et, s0_all, *, layer, prev_states):
    nblk, _, tb = gqt.shape
    tr = pl.BlockSpec((1, GK_W, tb), lambda i: (i, 0, 0))
    row = pl.BlockSpec((tb, GV_W), lambda i: (i, 0))
    st_in = pl.BlockSpec((1, tb, GLA_HEADS, GLA_DK, GLA_DV), lambda i: (layer, i, 0, 0, 0))
    st = lambda nl: pl.BlockSpec((nl, tb, GLA_HEADS, GLA_DK, GLA_DV), lambda i: (0, i, 0, 0, 0))
    prev = () if prev_states is None else (prev_states,)
    return pl.pallas_call(
        functools.partial(_gla_sample_kernel, tb=tb),
        grid=(nblk,),
        in_specs=[tr, tr, row, tr, st_in] + ([st(layer)] if prev else []),
        out_specs=(row, st(layer + 1)),
        out_shape=(jax.ShapeDtypeStruct((nblk * tb, GV_W), f32),
                   jax.ShapeDtypeStruct((layer + 1,) + s0_all.shape[1:], f32)),
        compiler_params=_cparams(("parallel",)),
        name="gla_sample",
    )(gqt, gkt, gv, et, s0_all, *prev)


def _rope_tables(pos):
    half = DIFF_DK // 2
    freqs = ROPE_THETA ** (-jnp.arange(half, dtype=f32) / half)
    ang = pos.astype(f32)[:, None] * freqs[None, :]
    cos, sin = jnp.cos(ang), jnp.sin(ang)
    reps = LANES // DIFF_DK
    return jnp.tile(jnp.concatenate([cos, cos], -1), (1, reps)), jnp.tile(jnp.concatenate([-sin, sin], -1), (1, reps))


def kernel(x_prompt, x_sample, cache_k, cache_v, state_gla, page_table, norm1, w_in, q_norm, k_norm, lambda_qk,
           subln, w_a2, b_a, gla_norm, w_out, norm2, w_gu, w_down):
    batch, seq, d = x_prompt.shape
    nb, dec_seq, _ = x_sample.shape
    assert dec_seq == 1
    depth = w_in.shape[0]
    n_pages = page_table.shape[1]
    page = cache_k.shape[2]
    past = n_pages * page
    rank = w_a2.shape[1]

    cos_p, sin_p = _rope_tables(jnp.arange(seq))
    cos_s, sin_s = _rope_tables(past + jnp.arange(dec_seq))
    cos_s = jnp.broadcast_to(cos_s, (nb, LANES))
    sin_s = jnp.broadcast_to(sin_s, (nb, LANES))

    ck = jnp.transpose(cache_k, (0, 1, 3, 4, 5, 2))
    cv = cache_v.reshape(depth, -1, page * DIFF_HEADS, DIFF_DV)

    yp = x_prompt.reshape(batch * seq, d)
    ys = x_sample.reshape(nb, d)
    w_in_b = w_in.astype(bf16)
    w_r = jnp.pad(w_in[:, :, MAIN_W:], ((0, 0), (0, 0), (0, LANES - rank))).astype(bf16)
    wa2 = jnp.pad(w_a2, ((0, 0), (0, LANES - rank), (0, 0))).astype(bf16)
    w_out_b, w_gu_b, w_down_b = w_out.astype(bf16), w_gu.astype(bf16), w_down.astype(bf16)

    sp, k_s, v_s = [], [], []
    kv_prompt = None
    states_s = None
    for l in range(depth):
        lam_init = 0.8 - 0.6 * math.exp(-0.3 * l)
        proj_args = (norm1[l][None], w_in_b, w_r, wa2, b_a[l][None],
                     jnp.tile(q_norm[l], QK_W // DIFF_DK)[None], jnp.tile(k_norm[l], QK_W // DIFF_DK)[None])
        post_args = (jnp.tile(subln[l], DIFF_HEADS)[None], jnp.tile(gla_norm[l], GLA_HEADS)[None],
                     w_out_b, norm2[l][None], w_gu_b, w_down_b)

        q, kb, kt, vb, v4, gq, gk, gv, gl, gg = _proj_prompt(yp, *proj_args, cos_p, sin_p, layer=l, batch=batch,
                                                             tm=ROW_TILE, prev_kv=kv_prompt)
        kv_prompt = (kt, v4)
        q_s, k_s_l, v_s_l, gqt, gkt, gv_s, et, gg_s = _proj_sample(ys, *proj_args, cos_s, sin_s, layer=l,
                                                                   tb=GLA_SAMPLE_BLOCK)
        od = _attn_prompt(q, kb, vb, lambda_qk[l], batch=batch, seq=seq, tq=ATTN_Q_TILE, heads=ATTN_HEADS_PER_STEP,
                          lam_init=lam_init)
        og, s_fin = _gla_prompt(gq, gk, gv, gl, batch=batch, seq=seq, tg=GLA_STEP_TOKENS)
        yp, od_s = _post_attn(yp, od, og, gg, *post_args, page_table, q_s, k_s_l, v_s_l, lambda_qk[l], ck, cv,
                              tm=ROW_TILE, lam_init=lam_init, layer=l)
        sp.append(s_fin)

        og_s, states_s = _gla_sample(gqt, gkt, gv_s, et, state_gla, layer=l, prev_states=states_s)
        ys = _post(ys, od_s, og_s, gg_s, *post_args, layer=l, tm=nb, lam_init=lam_init)
        k_s.append(k_s_l.reshape(nb, dec_seq, DIFF_HEADS, 2, DIFF_DK))
        v_s.append(v_s_l.reshape(nb, dec_seq, DIFF_HEADS, DIFF_DV))

    k_prompt = kv_prompt[0].reshape(depth, batch, DIFF_HEADS, 2, DIFF_DK, seq).transpose(0, 1, 5, 2, 3, 4)
    v_prompt = kv_prompt[1].reshape(depth, batch, seq, DIFF_HEADS, DIFF_DV)
    return (yp.reshape(batch, seq, d), ys.reshape(nb, dec_seq, d), k_prompt, v_prompt, jnp.stack(sp),
            jnp.stack(k_s), jnp.stack(v_s), states_s)
```

```python
import functools
import math

import jax
import jax.numpy as jnp
from jax import lax
from jax.experimental import pallas as pl
from jax.experimental.pallas import tpu as pltpu

f32 = jnp.float32
bf16 = jnp.bfloat16

DIFF_HEADS = 4
DIFF_DK = 64
DIFF_DV = 128
GLA_HEADS = 4
GLA_DK = 64
GLA_DV = 128
GLA_GATE_NORM = 16.0
ROPE_THETA = 10000.0
EPS = 1e-6
LOG2_E = math.log2(math.e)

QK_W = DIFF_HEADS * 2 * DIFF_DK
DV_W = DIFF_HEADS * DIFF_DV
GK_W = GLA_HEADS * GLA_DK
GV_W = GLA_HEADS * GLA_DV
MAIN_W = 2 * QK_W + DV_W + 2 * GK_W + 2 * GV_W

LANES = 128
ROW_TILE = 512
ATTN_Q_TILE = 256
ATTN_HEADS_PER_STEP = 2
GLA_STEP_TOKENS = 1024
GLA_SAMPLE_BLOCK = 32
GLA_BLOCK = 256
GLA_SAFE_LOG_DECAY = 80.0
VMEM_LIMIT = 56 * 1024 * 1024
FUSED_VMEM_LIMIT = 60 * 1024 * 1024


def _cparams(sem):
    return pltpu.CompilerParams(dimension_semantics=sem, vmem_limit_bytes=VMEM_LIMIT)


def _const_spec(shape):
    nd = len(shape)
    return pl.BlockSpec(shape, lambda *_: (0,) * nd, pipeline_mode=pl.Buffered(1))


def _layer_spec(shape, layer):
    nd = len(shape)
    return pl.BlockSpec((1,) + tuple(shape[1:]), lambda *_: (layer,) + (0,) * (nd - 1), pipeline_mode=pl.Buffered(1))


def _group_scale(z, width):
    lane = lax.broadcasted_iota(jnp.int32, (z.shape[0], LANES), 1)
    cols = []
    for c in range(z.shape[1] // LANES):
        zc = z[:, c * LANES:(c + 1) * LANES]
        zz = zc * zc
        if width == LANES:
            cols.append(jnp.broadcast_to(lax.rsqrt(jnp.mean(zz, axis=-1, keepdims=True) + EPS), zc.shape))
        else:
            lo = lane < width
            s_lo = jnp.sum(jnp.where(lo, zz, 0.0), axis=-1, keepdims=True)
            s_hi = jnp.sum(jnp.where(lo, 0.0, zz), axis=-1, keepdims=True)
            r_lo = lax.rsqrt(s_lo * (1.0 / width) + EPS)
            r_hi = lax.rsqrt(s_hi * (1.0 / width) + EPS)
            cols.append(jnp.where(lo, r_lo, r_hi))
    return jnp.concatenate(cols, axis=-1)


def _rope(z, cos, sin_signed):
    lane = lax.broadcasted_iota(jnp.int32, (z.shape[0], LANES), 1)
    first_half = (lane % DIFF_DK) < (DIFF_DK // 2)
    cols = []
    for c in range(z.shape[1] // LANES):
        zc = z[:, c * LANES:(c + 1) * LANES]
        partner = jnp.where(first_half, pltpu.roll(zc, LANES - DIFF_DK // 2, 1), pltpu.roll(zc, DIFF_DK // 2, 1))
        cols.append(zc * cos + partner * sin_signed)
    return jnp.concatenate(cols, axis=-1)


def _diff_lambda(lqk, lam_init):
    a = jnp.sum(lqk[0:1, :] * lqk[1:2, :], axis=-1, keepdims=True)
    b = jnp.sum(lqk[2:3, :] * lqk[3:4, :], axis=-1, keepdims=True)
    return jnp.exp(a) - jnp.exp(b) + lam_init


def _proj_body(x, g1_ref, w_ref, wr_ref, wa2_ref, ba_ref, qn_ref, kn_ref, cos, sin):
    n = x * lax.rsqrt(jnp.mean(x * x, axis=-1, keepdims=True) + EPS) * g1_ref[...]
    nb = n.astype(bf16)

    def seg(lo, width):
        return jnp.dot(nb, w_ref[0, :, lo:lo + width], preferred_element_type=f32)

    zq = seg(0, QK_W)
    q = _rope(zq * _group_scale(zq, DIFF_DK) * qn_ref[...], cos, sin) * (DIFF_DK ** -0.5 * LOG2_E)
    zk = seg(QK_W, QK_W)
    k = _rope(zk * _group_scale(zk, DIFF_DK) * kn_ref[...], cos, sin)
    v = seg(2 * QK_W, DV_W)
    off = 2 * QK_W + DV_W
    gq = seg(off, GK_W) * (GLA_DK ** -0.5)
    gk = seg(off + GK_W, GK_W)
    gv = seg(off + 2 * GK_W, GV_W)
    gg = seg(off + 2 * GK_W + GV_W, GV_W)
    r = jnp.dot(nb, wr_ref[0], preferred_element_type=f32)
    a = jnp.dot(r.astype(bf16), wa2_ref[0], preferred_element_type=f32) + ba_ref[...]
    gl = (jnp.minimum(a, 0.0) - jnp.log1p(jnp.exp(-jnp.abs(a)))) * (1.0 / GLA_GATE_NORM)
    return q, k, v, gq, gk, gv, gl, gg


def _proj_prompt_kernel(x_ref, g1_ref, w_ref, wr_ref, wa2_ref, ba_ref, qn_ref, kn_ref, cos_ref, sin_ref, *rest):
    q_ref, kb_ref, kt_ref, vb_ref, v4_ref, gq_ref, gk_ref, gv_ref, gl_ref, gg_ref = rest[-10:]
    layer = kt_ref.shape[0] - 1
    if layer:
        prev_kt_ref, prev_v4_ref = rest[:2]
        kt_ref[:layer] = prev_kt_ref[...]
        v4_ref[:layer] = prev_v4_ref[...]
    tm = x_ref.shape[0]
    q, k, v, gq, gk, gv, gl, gg = _proj_body(x_ref[...], g1_ref, w_ref, wr_ref, wa2_ref, ba_ref, qn_ref, kn_ref,
                                             cos_ref[...], sin_ref[...])
    first_map = lax.broadcasted_iota(jnp.int32, (tm, LANES), 1) < DIFF_DK
    for h in range(DIFF_HEADS):
        qh = q[:, h * LANES:(h + 1) * LANES]
        q_ref[:, (2 * h) * LANES:(2 * h + 1) * LANES] = jnp.where(first_map, qh, 0.0).astype(bf16)
        q_ref[:, (2 * h + 1) * LANES:(2 * h + 2) * LANES] = jnp.where(first_map, 0.0, qh).astype(bf16)
    kb_ref[...] = k.astype(bf16)
    kt_ref[layer, 0] = k.T
    vb_ref[...] = v.astype(bf16)
    for h in range(DIFF_HEADS):
        v4_ref[layer, pl.ds(h, tm, stride=DIFF_HEADS), :] = v[:, h * DIFF_DV:(h + 1) * DIFF_DV]
    gq_ref[...] = gq
    gk_ref[...] = gk
    gv_ref[...] = gv.astype(bf16)
    gl_ref[...] = gl
    gg_ref[...] = gg


def _proj_prompt(x2d, g1, w_main, w_r, w_a2, b_a, qn, kn, cos_t, sin_t, *, layer, batch, tm, prev_kv):
    n, d = x2d.shape
    seq = n // batch
    nt = seq // tm
    row = lambda w: pl.BlockSpec((tm, w), lambda b, j: (b * nt + j, 0))
    tab = pl.BlockSpec((tm, LANES), lambda b, j: (j, 0))
    arr = lambda w, dt: jax.ShapeDtypeStruct((n, w), dt)
    kt_spec = lambda nl: pl.BlockSpec((nl, 1, QK_W, tm), lambda b, j: (0, b, 0, j))
    v4_spec = lambda nl: pl.BlockSpec((nl, tm * DIFF_HEADS, DIFF_DV), lambda b, j: (0, b * nt + j, 0))
    out_shape = (
        arr(2 * QK_W, bf16),
        arr(QK_W, bf16),
        jax.ShapeDtypeStruct((layer + 1, batch, QK_W, seq), f32),
        arr(DV_W, bf16),
        jax.ShapeDtypeStruct((layer + 1, n * DIFF_HEADS, DIFF_DV), f32),
        arr(GK_W, f32), arr(GK_W, f32), arr(GV_W, bf16), arr(GK_W, f32), arr(GV_W, f32),
    )
    out_specs = (row(2 * QK_W), row(QK_W), kt_spec(layer + 1), row(DV_W), v4_spec(layer + 1),
                 row(GK_W), row(GK_W), row(GV_W), row(GK_W), row(GV_W))
    prev = () if prev_kv is None else tuple(prev_kv)
    return pl.pallas_call(
        _proj_prompt_kernel,
        grid=(batch, nt),
        in_specs=[row(d), _const_spec((1, d)), _layer_spec(w_main.shape, layer), _layer_spec(w_r.shape, layer),
                  _layer_spec(w_a2.shape, layer), _const_spec((1, GK_W)), _const_spec((1, QK_W)),
                  _const_spec((1, QK_W)), tab, tab] + ([kt_spec(layer), v4_spec(layer)] if prev else []),
        out_specs=out_specs,
        out_shape=out_shape,
        compiler_params=_cparams(("parallel", "parallel")),
        name="proj_prompt",
    )(x2d, g1, w_main, w_r, w_a2, b_a, qn, kn, cos_t, sin_t, *prev)


def _proj_sample_kernel(x_ref, g1_ref, w_ref, wr_ref, wa2_ref, ba_ref, qn_ref, kn_ref, cos_ref, sin_ref,
                        q_ref, k_ref, v_ref, gqt_ref, gkt_ref, gv_ref, et_ref, gg_ref):
    q, k, v, gq, gk, gv, gl, gg = _proj_body(x_ref[...], g1_ref, w_ref, wr_ref, wa2_ref, ba_ref, qn_ref, kn_ref,
                                             cos_ref[...], sin_ref[...])
    q_ref[...] = q
    k_ref[...] = k
    v_ref[...] = v
    gv_ref[...] = gv
    gg_ref[...] = gg
    e = jnp.exp(gl)
    tb = gqt_ref.shape[2]
    for blk in range(gqt_ref.shape[0]):
        rows = slice(blk * tb, (blk + 1) * tb)
        gqt_ref[blk] = gq[rows, :].T
        gkt_ref[blk] = gk[rows, :].T
        et_ref[blk] = e[rows, :].T


def _proj_sample(x2d, g1, w_main, w_r, w_a2, b_a, qn, kn, cos_t, sin_t, *, layer, tb):
    n, d = x2d.shape
    flat = lambda w: jax.ShapeDtypeStruct((n, w), f32)
    tr = jax.ShapeDtypeStruct((n // tb, GK_W, tb), f32)
    out_shape = (flat(QK_W), flat(QK_W), flat(DV_W), tr, tr, flat(GV_W), tr, flat(GV_W))
    whole = lambda shape: pl.BlockSpec(shape, lambda i: (0,) * len(shape))
    return pl.pallas_call(
        _proj_sample_kernel,
        grid=(1,),
        in_specs=[whole(x2d.shape), whole((1, d)), _layer_spec(w_main.shape, layer), _layer_spec(w_r.shape, layer),
                  _layer_spec(w_a2.shape, layer), whole((1, GK_W)), whole((1, QK_W)), whole((1, QK_W)),
                  whole(cos_t.shape), whole(sin_t.shape)],
        out_specs=tuple(whole(o.shape) for o in out_shape),
        out_shape=out_shape,
        compiler_params=_cparams(("arbitrary",)),
        name="proj_sample",
    )(x2d, g1, w_main, w_r, w_a2, b_a, qn, kn, cos_t, sin_t)


def _attn_kernel(q_ref, kb_ref, v_ref, lqk_ref, o_ref, vb_ref, s_ref, *, tq, heads, lam_init):
    seq = kb_ref.shape[0]
    lam = _diff_lambda(lqk_ref[...], lam_init)
    row = lax.broadcasted_iota(jnp.int32, (2 * tq, tq), 0) % tq
    col = lax.broadcasted_iota(jnp.int32, (2 * tq, tq), 1)
    visible = col <= row
    nq = seq // tq
    for hh in range(heads):
        kcols = slice(hh * LANES, (hh + 1) * LANES)
        vb_ref[hh, :, :DIFF_DV] = v_ref[:, kcols]
        vb_ref[hh, :, DIFF_DV:] = jnp.ones((seq, LANES), bf16)
        for qi in (range(nq) if hh % 2 == 0 else reversed(range(nq))):
            rows = slice(qi * tq, (qi + 1) * tq)
            qs = jnp.concatenate([q_ref[rows, (2 * hh) * LANES:(2 * hh + 1) * LANES],
                                  q_ref[rows, (2 * hh + 1) * LANES:(2 * hh + 2) * LANES]], axis=0)
            mx = None
            for kt in range(qi + 1):
                cols = slice(kt * tq, (kt + 1) * tq)
                s = lax.dot_general(qs, kb_ref[cols, kcols], (((1,), (1,)), ((), ())), preferred_element_type=f32)
                if kt == qi:
                    s = jnp.where(visible, s, -jnp.inf)
                s_ref[hh, :, cols] = s
                for c in range(tq // LANES):
                    sc = s[:, c * LANES:(c + 1) * LANES]
                    mx = sc if mx is None else jnp.maximum(mx, sc)
            m = jnp.broadcast_to(jnp.max(mx, axis=-1, keepdims=True), (2 * tq, tq))
            acc = None
            for kt in range(qi + 1):
                cols = slice(kt * tq, (kt + 1) * tq)
                p = jnp.exp2(s_ref[hh, :, cols] - m).astype(bf16)
                pv = jnp.dot(p, vb_ref[hh, cols, :], preferred_element_type=f32)
                acc = pv if acc is None else acc + pv
            o1 = acc[:tq, :DIFF_DV] / acc[:tq, DIFF_DV:]
            o2 = acc[tq:, :DIFF_DV] / acc[tq:, DIFF_DV:]
            o_ref[rows, hh * DIFF_DV:(hh + 1) * DIFF_DV] = o1 - lam * o2


def _attn_prompt(q, kb, vb, lqk, *, batch, seq, tq, heads, lam_init):
    return pl.pallas_call(
        functools.partial(_attn_kernel, tq=tq, heads=heads, lam_init=lam_init),
        grid=(batch, DIFF_HEADS // heads),
        in_specs=[pl.BlockSpec((seq, heads * 2 * LANES), lambda b, g: (b, g)),
                  pl.BlockSpec((seq, heads * LANES), lambda b, g: (b, g)),
                  pl.BlockSpec((seq, heads * LANES), lambda b, g: (b, g)),
                  _const_spec(lqk.shape)],
        out_specs=pl.BlockSpec((seq, heads * DIFF_DV), lambda b, g: (b, g)),
        out_shape=jax.ShapeDtypeStruct((batch * seq, DV_W), f32),
        scratch_shapes=[pltpu.VMEM((heads, seq, 2 * LANES), bf16), pltpu.VMEM((heads, 2 * tq, seq), f32)],
        compiler_params=_cparams(("parallel", "parallel")),
        name="attn_prompt",
    )(q, kb, vb, lqk)


def _cumsum_rows(g, tril):
    g1 = g.astype(bf16)
    r1 = g - g1.astype(f32)
    g2 = r1.astype(bf16)
    g3 = (r1 - g2.astype(f32)).astype(bf16)
    dot = lambda t: jnp.dot(tril, t, preferred_element_type=f32)
    return dot(g1) + dot(g2) + dot(g3)


def _gla_kernel(q_ref, k_ref, v_ref, g_ref, o_ref, s_ref, st_ref, b_ref, *, tg):
    step = pl.program_id(1)

    @pl.when(step == 0)
    def _():
        st_ref[...] = jnp.zeros(st_ref.shape, f32)

    c = GLA_BLOCK
    nblock = tg // c
    ri = lax.broadcasted_iota(jnp.int32, (c, c), 0)
    ci = lax.broadcasted_iota(jnp.int32, (c, c), 1)
    causal = ri >= ci
    tril = jnp.where(causal, 1.0, 0.0).astype(bf16)
    head0 = lax.broadcasted_iota(jnp.int32, (c, LANES), 1) < GLA_DK
    lane_sq = lax.broadcasted_iota(jnp.int32, (LANES, LANES), 1) < GLA_DK
    block_rows = lambda ib: slice(ib * c, (ib + 1) * c)
    head_cols = lambda h: slice(h * GLA_DV, (h + 1) * GLA_DV)
    head_mask = lambda j: head0 if j == 0 else jnp.logical_not(head0)

    worst = None
    for ib in range(nblock):
        b = _cumsum_rows(g_ref[block_rows(ib), :], tril)
        b_ref[block_rows(ib), :] = b
        b_mid = b[c // 2 - 1:c // 2, :]
        w = jnp.maximum(-b_mid, b_mid - b[c - 1:c, :])
        worst = w if worst is None else jnp.maximum(worst, w)
    safe = jnp.max(worst) < GLA_SAFE_LOG_DECAY

    @pl.when(safe)
    def _():
        for ib in range(nblock):
            rows = block_rows(ib)
            b = b_ref[rows, :]
            b_mid = b[c // 2 - 1:c // 2, :]
            qe = q_ref[rows, :] * jnp.exp(b - b_mid)
            ke = (k_ref[rows, :] * jnp.exp(b_mid - b)).astype(bf16)
            for h in range(GLA_HEADS):
                sl = slice((h // 2) * LANES, (h // 2 + 1) * LANES)
                qe_h = jnp.where(head_mask(h % 2), qe[:, sl], 0.0).astype(bf16)
                a = lax.dot_general(qe_h, ke[:, sl], (((1,), (1,)), ((), ())), preferred_element_type=f32)
                a = jnp.where(causal, a, 0.0).astype(bf16)
                o_ref[rows, head_cols(h)] = jnp.dot(a, v_ref[rows, head_cols(h)], preferred_element_type=f32)

    @pl.when(jnp.logical_not(safe))
    def _():
        tok = lax.broadcasted_iota(jnp.int32, (c, 1), 0)

        def exact_block(ib, carry):
            rows = pl.ds(pl.multiple_of(ib * c, c), c)
            b = b_ref[rows, :]
            q = q_ref[rows, :]
            k = k_ref[rows, :]
            v = v_ref[rows, :].astype(f32)
            pick = lambda x, s: jnp.sum(jnp.where(tok == s, x, 0.0), axis=0, keepdims=True)

            def key(s, acc):
                w = q * pick(k, s) * jnp.exp(jnp.minimum(b - pick(b, s), 0.0))
                w = jnp.where(tok >= s, w, 0.0)
                vs = pick(v, s)
                return tuple(
                    acc[h] + jnp.sum(w[:, h * GLA_DK:(h + 1) * GLA_DK], axis=-1, keepdims=True) * vs[:, head_cols(h)]
                    for h in range(GLA_HEADS))

            acc = lax.fori_loop(0, c, key, tuple(jnp.zeros((c, GLA_DV), f32) for _ in range(GLA_HEADS)))
            for h in range(GLA_HEADS):
                o_ref[rows, head_cols(h)] = acc[h]
            return carry

        lax.fori_loop(0, nblock, exact_block, 0)

    for ib in range(nblock):
        rows = block_rows(ib)
        b = b_ref[rows, :]
        b_last = b[c - 1:c, :]
        qb = q_ref[rows, :] * jnp.exp(b)
        kd = (k_ref[rows, :] * jnp.exp(b_last - b)).astype(bf16)
        e_last = jnp.exp(b_last)
        for p in range(GLA_HEADS // 2):
            sl = slice(p * LANES, (p + 1) * LANES)
            st = st_ref[p]
            stb = st.astype(bf16)
            upd = []
            for j in range(2):
                h = 2 * p + j
                qb_h = jnp.where(head_mask(j), qb[:, sl], 0.0).astype(bf16)
                o_ref[rows, head_cols(h)] += lax.dot_general(qb_h, stb, (((1,), (1,)), ((), ())),
                                                             preferred_element_type=f32)
                upd.append(lax.dot_general(v_ref[rows, head_cols(h)], kd[:, sl], (((0,), (0,)), ((), ())),
                                           preferred_element_type=f32))
            st_ref[p] = st * e_last[:, sl] + jnp.where(lane_sq, upd[0], upd[1])

    @pl.when(step == pl.num_programs(1) - 1)
    def _():
        for p in range(GLA_HEADS // 2):
            t = st_ref[p].T
            s_ref[0, 2 * p] = t[:GLA_DK, :]
            s_ref[0, 2 * p + 1] = t[GLA_DK:, :]


def _gla_prompt(gq, gk, gv, gl, *, batch, seq, tg):
    ns = seq // tg
    row = lambda w: pl.BlockSpec((tg, w), lambda b, i: (b * ns + i, 0))
    return pl.pallas_call(
        functools.partial(_gla_kernel, tg=tg),
        grid=(batch, ns),
        in_specs=[row(GK_W), row(GK_W), row(GV_W), row(GK_W)],
        out_specs=(row(GV_W), pl.BlockSpec((1, GLA_HEADS, GLA_DK, GLA_DV), lambda b, i: (b, 0, 0, 0))),
        out_shape=(jax.ShapeDtypeStruct((batch * seq, GV_W), f32),
                   jax.ShapeDtypeStruct((batch, GLA_HEADS, GLA_DK, GLA_DV), f32)),
        scratch_shapes=[pltpu.VMEM((GLA_HEADS // 2, LANES, LANES), f32), pltpu.VMEM((tg, GK_W), f32)],
        compiler_params=_cparams(("parallel", "arbitrary")),
        name="gla_prompt",
    )(gq, gk, gv, gl)


def _merge_ffn_pieces(x_ref, od_ref, og_ref, gg_ref, sub_ref, gn_ref, wo_ref, n2_ref, lam_init):
    od = od_ref[...]
    odn = od * _group_scale(od, DIFF_DV) * sub_ref[...] * (1.0 - lam_init)
    og = og_ref[...]
    ogn = og * _group_scale(og, GLA_DV) * gn_ref[...]
    gg = gg_ref[...]
    ogn = ogn * (gg * jax.nn.sigmoid(gg))
    mix = jnp.concatenate([odn, ogn], axis=-1).astype(bf16)
    y = x_ref[...] + jnp.dot(mix, wo_ref[0], preferred_element_type=f32)
    n2 = (y * lax.rsqrt(jnp.mean(y * y, axis=-1, keepdims=True) + EPS) * n2_ref[...]).astype(bf16)
    return y, n2


def _ffn_chunk(n2, wgu_ref, wd_ref, c0, ff_chunk, d_ff):
    a = jnp.dot(n2, wgu_ref[0, :, c0:c0 + ff_chunk], preferred_element_type=f32)
    b = jnp.dot(n2, wgu_ref[0, :, d_ff + c0:d_ff + c0 + ff_chunk], preferred_element_type=f32)
    hid = (a * jax.nn.sigmoid(a) * b).astype(bf16)
    return jnp.dot(hid, wd_ref[0, c0:c0 + ff_chunk, :], preferred_element_type=f32)


def _post_kernel(x_ref, od_ref, og_ref, gg_ref, sub_ref, gn_ref, wo_ref, n2_ref, wgu_ref, wd_ref, y_ref,
                 *, lam_init, d_ff, ff_chunk):
    acc, n2 = _merge_ffn_pieces(x_ref, od_ref, og_ref, gg_ref, sub_ref, gn_ref, wo_ref, n2_ref, lam_init)
    for c0 in range(0, d_ff, ff_chunk):
        acc = acc + _ffn_chunk(n2, wgu_ref, wd_ref, c0, ff_chunk, d_ff)
    y_ref[...] = acc


def _post(x2d, od, og, gg, sub, gn, w_out, n2, w_gu, w_down, *, layer, tm, lam_init):
    n, d = x2d.shape
    d_ff = w_down.shape[1]
    row = lambda w: pl.BlockSpec((tm, w), lambda i: (i, 0))
    return pl.pallas_call(
        functools.partial(_post_kernel, lam_init=lam_init, d_ff=d_ff, ff_chunk=2 * LANES),
        grid=(n // tm,),
        in_specs=[row(d), row(DV_W), row(GV_W), row(GV_W), _const_spec((1, DV_W)), _const_spec((1, GV_W)),
                  _layer_spec(w_out.shape, layer), _const_spec((1, d)), _layer_spec(w_gu.shape, layer),
                  _layer_spec(w_down.shape, layer)],
        out_specs=row(d),
        out_shape=jax.ShapeDtypeStruct((n, d), f32),
        compiler_params=_cparams(("parallel",)),
        name="post",
    )(x2d, od, og, gg, sub, gn, w_out, n2, w_gu, w_down)


def _sample_probs(q, k_new, kbuf, slot, *, n_pages, page):
    past = n_pages * page
    nrow = 2 * DIFF_HEADS
    qb = q.astype(bf16)
    rid = lax.broadcasted_iota(jnp.int32, (nrow, past), 0)
    s = jnp.zeros((nrow, past), f32)
    for h in range(DIFF_HEADS):
        for mp in range(2):
            kt = jnp.concatenate([kbuf[slot, j, h, mp] for j in range(n_pages)], axis=-1)
            sr = jnp.dot(qb, kt.astype(bf16), preferred_element_type=f32)
            s = jnp.where(rid == 2 * h + mp, sr, s)
    s_new = jnp.sum(q * k_new, axis=-1, keepdims=True)
    m = jnp.maximum(jnp.max(s, axis=-1, keepdims=True), s_new)
    p = jnp.exp2(s - m)
    p_new = jnp.exp2(s_new - m)
    inv_l = 1.0 / (jnp.sum(p, axis=-1, keepdims=True) + p_new)
    return p.astype(bf16), p_new, inv_l


def _sample_values(pb, p_new, inv_l, v_new, lam, vbuf, slot, *, n_pages, page):
    def head_values(h):
        vh = jnp.concatenate([vbuf[slot, j, pl.ds(h, page, stride=DIFF_HEADS), :] for j in range(n_pages)], axis=0)
        return vh.astype(bf16)

    heads = []
    for h0 in range(0, DIFF_HEADS, 2):
        pair = jnp.dot(pb, jnp.concatenate([head_values(h0), head_values(h0 + 1)], axis=-1),
                       preferred_element_type=f32)
        for h in (h0, h0 + 1):
            oh = pair[:, (h - h0) * DIFF_DV:(h - h0 + 1) * DIFF_DV]
            oh = (oh + p_new * v_new[:, h * DIFF_DV:(h + 1) * DIFF_DV]) * inv_l
            heads.append(oh[2 * h:2 * h + 1, :] - lam * oh[2 * h + 1:2 * h + 2, :])
    return jnp.concatenate(heads, axis=-1)


def _page_copies(pt_ref, cache_hbm, buf, sem, which, sample, slot, *, layer, n_pages):
    return [pltpu.make_async_copy(cache_hbm.at[layer, pt_ref[sample * n_pages + j]], buf.at[slot, j],
                                  sem.at[which, slot]) for j in range(n_pages)]


def _post_attn_kernel(pt_ref, x_ref, od_ref, og_ref, gg_ref, sub_ref, gn_ref, wo_ref, n2_ref, wgu_ref, wd_ref,
                      q_ref, kn_ref, vn_ref, lqk_ref, ck_hbm, cv_hbm, y_ref, os_ref, kbuf, vbuf, sem,
                      *, lam_init, d_ff, ff_chunk, layer, n_pages, page, per_step):
    step = pl.program_id(0)
    nsteps = pl.num_programs(0)
    k_copies = functools.partial(_page_copies, pt_ref, ck_hbm, kbuf, sem, 0, layer=layer, n_pages=n_pages)
    v_copies = functools.partial(_page_copies, pt_ref, cv_hbm, vbuf, sem, 1, layer=layer, n_pages=n_pages)

    nb = nsteps * per_step

    def start_ahead(copies, j):
        g2 = step * per_step + j + 2
        if j + 2 < per_step:
            for cp in copies(g2, j % 2):
                cp.start()
        else:
            @pl.when(g2 < nb)
            def _():
                for cp in copies(g2, j % 2):
                    cp.start()

    @pl.when(step == 0)
    def _():
        for g in range(2):
            for cp in k_copies(g, g) + v_copies(g, g):
                cp.start()

    lam = _diff_lambda(lqk_ref[...], lam_init)

    def sample_probs(j):
        g = step * per_step + j
        slot = j % 2
        for cp in k_copies(g, slot) + v_copies(g, slot):
            cp.wait()
        probs = _sample_probs(q_ref[j], kn_ref[j], kbuf, slot, n_pages=n_pages, page=page)
        start_ahead(k_copies, j)
        return probs

    def sample_values(j, probs):
        os_ref[j] = _sample_values(*probs, vn_ref[j], lam, vbuf, j % 2, n_pages=n_pages, page=page)
        start_ahead(v_copies, j)

    acc, n2 = _merge_ffn_pieces(x_ref, od_ref, og_ref, gg_ref, sub_ref, gn_ref, wo_ref, n2_ref, lam_init)
    nchunk = d_ff // ff_chunk
    sample_at = {(j * nchunk) // per_step: j for j in range(per_step)}
    for ic in range(nchunk):
        if ic in sample_at:
            probs = sample_probs(sample_at[ic])
        acc = acc + _ffn_chunk(n2, wgu_ref, wd_ref, ic * ff_chunk, ff_chunk, d_ff)
        if ic in sample_at:
            sample_values(sample_at[ic], probs)
    y_ref[...] = acc


def _post_attn(x2d, od, og, gg, sub, gn, w_out, n2, w_gu, w_down, page_table, q_s, kn_s, vn_s, lqk, cache_kt, cache_vr,
               *, tm, lam_init, layer):
    n, d = x2d.shape
    d_ff = w_down.shape[1]
    nsteps = n // tm
    nb, n_pages = page_table.shape
    per_step = nb // nsteps
    assert per_step * nsteps == nb and per_step % 2 == 0
    page = cache_kt.shape[-1]
    nrow = 2 * DIFF_HEADS
    row = lambda w: pl.BlockSpec((tm, w), lambda i, pt: (i, 0))
    const = lambda shape: pl.BlockSpec(shape, lambda i, pt: (0,) * len(shape), pipeline_mode=pl.Buffered(1))
    per = lambda r, w: pl.BlockSpec((per_step, r, w), lambda i, pt: (i, 0, 0))
    grid_spec = pltpu.PrefetchScalarGridSpec(
        num_scalar_prefetch=1,
        grid=(nsteps,),
        in_specs=[row(d), row(DV_W), row(GV_W), row(GV_W), const((1, DV_W)), const((1, GV_W)),
                  _layer_spec(w_out.shape, layer), const((1, d)), _layer_spec(w_gu.shape, layer),
                  _layer_spec(w_down.shape, layer),
                  per(nrow, DIFF_DK), per(nrow, DIFF_DK), per(1, DV_W), const(lqk.shape),
                  pl.BlockSpec(memory_space=pl.ANY), pl.BlockSpec(memory_space=pl.ANY)],
        out_specs=(row(d), per(1, DV_W)),
        scratch_shapes=[pltpu.VMEM((2, n_pages) + cache_kt.shape[2:], f32),
                        pltpu.VMEM((2, n_pages) + cache_vr.shape[2:], f32),
                        pltpu.SemaphoreType.DMA((2, 2))],
    )
    y, o_s = pl.pallas_call(
        functools.partial(_post_attn_kernel, lam_init=lam_init, d_ff=d_ff, ff_chunk=2 * LANES, layer=layer,
                          n_pages=n_pages, page=page, per_step=per_step),
        grid_spec=grid_spec,
        out_shape=(jax.ShapeDtypeStruct((n, d), f32), jax.ShapeDtypeStruct((nb, 1, DV_W), f32)),
        compiler_params=pltpu.CompilerParams(dimension_semantics=("arbitrary",), vmem_limit_bytes=FUSED_VMEM_LIMIT),
        name="post_attn",
    )(page_table.reshape(-1), x2d, od, og, gg, sub, gn, w_out, n2, w_gu, w_down,
      q_s.reshape(nb, nrow, DIFF_DK), kn_s.reshape(nb, nrow, DIFF_DK), vn_s.reshape(nb, 1, DV_W), lqk,
      cache_kt, cache_vr)
    return y, o_s.reshape(nb, DV_W)


def _gla_sample_kernel(qt_ref, kt_ref, v_ref, et_ref, s0_ref, *rest, tb):
    o_ref, s_ref = rest[-2:]
    layer = s_ref.shape[0] - 1
    if layer:
        s_ref[:layer] = rest[0][...]
    qt = qt_ref[0]
    kt = kt_ref[0]
    et = et_ref[0]
    v = v_ref[...]
    for i in range(tb):
        for h in range(GLA_HEADS):
            ks = slice(h * GLA_DK, (h + 1) * GLA_DK)
            vs = slice(h * GLA_DV, (h + 1) * GLA_DV)
            s_new = s0_ref[0, i, h] * et[ks, i:i + 1] + kt[ks, i:i + 1] * v[i:i + 1, vs]
            s_ref[layer, i, h] = s_new
            o_ref[i:i + 1, vs] = jnp.sum(qt[ks, i:i + 1] * s_new, axis=0, keepdims=True)


def _gla_sample(gqt, gkt, gv, et, s0_all, *, layer, prev_states):
    nblk, _, tb = gqt.shape
    tr = pl.BlockSpec((1, GK_W, tb), lambda i: (i, 0, 0))
    row = pl.BlockSpec((tb, GV_W), lambda i: (i, 0))
    st_in = pl.BlockSpec((1, tb, GLA_HEADS, GLA_DK, GLA_DV), lambda i: (layer, i, 0, 0, 0))
    st = lambda nl: pl.BlockSpec((nl, tb, GLA_HEADS, GLA_DK, GLA_DV), lambda i: (0, i, 0, 0, 0))
    prev = () if prev_states is None else (prev_states,)
    return pl.pallas_call(
        functools.partial(_gla_sample_kernel, tb=tb),
        grid=(nblk,),
        in_specs=[tr, tr, row, tr, st_in] + ([st(layer)] if prev else []),
        out_specs=(row, st(layer + 1)),
        out_shape=(jax.ShapeDtypeStruct((nblk * tb, GV_W), f32),
                   jax.ShapeDtypeStruct((layer + 1,) + s0_all.shape[1:], f32)),
        compiler_params=_cparams(("parallel",)),
        name="gla_sample",
    )(gqt, gkt, gv, et, s0_all, *prev)


def _rope_tables(pos):
    half = DIFF_DK // 2
    freqs = ROPE_THETA ** (-jnp.arange(half, dtype=f32) / half)
    ang = pos.astype(f32)[:, None] * freqs[None, :]
    cos, sin = jnp.cos(ang), jnp.sin(ang)
    reps = LANES // DIFF_DK
    return jnp.tile(jnp.concatenate([cos, cos], -1), (1, reps)), jnp.tile(jnp.concatenate([-sin, sin], -1), (1, reps))


def kernel(x_prompt, x_sample, cache_k, cache_v, state_gla, page_table, norm1, w_in, q_norm, k_norm, lambda_qk,
           subln, w_a2, b_a, gla_norm, w_out, norm2, w_gu, w_down):
    batch, seq, d = x_prompt.shape
    nb, dec_seq, _ = x_sample.shape
    assert dec_seq == 1
    depth = w_in.shape[0]
    n_pages = page_table.shape[1]
    page = cache_k.shape[2]
    past = n_pages * page
    rank = w_a2.shape[1]

    cos_p, sin_p = _rope_tables(jnp.arange(seq))
    cos_s, sin_s = _rope_tables(past + jnp.arange(dec_seq))
    cos_s = jnp.broadcast_to(cos_s, (nb, LANES))
    sin_s = jnp.broadcast_to(sin_s, (nb, LANES))

    ck = jnp.transpose(cache_k, (0, 1, 3, 4, 5, 2))
    cv = cache_v.reshape(depth, -1, page * DIFF_HEADS, DIFF_DV)

    yp = x_prompt.reshape(batch * seq, d)
    ys = x_sample.reshape(nb, d)
    w_in_b = w_in.astype(bf16)
    w_r = jnp.pad(w_in[:, :, MAIN_W:], ((0, 0), (0, 0), (0, LANES - rank))).astype(bf16)
    wa2 = jnp.pad(w_a2, ((0, 0), (0, LANES - rank), (0, 0))).astype(bf16)
    w_out_b, w_gu_b, w_down_b = w_out.astype(bf16), w_gu.astype(bf16), w_down.astype(bf16)

    sp, k_s, v_s = [], [], []
    kv_prompt = None
    states_s = None
    for l in range(depth):
        lam_init = 0.8 - 0.6 * math.exp(-0.3 * l)
        proj_args = (norm1[l][None], w_in_b, w_r, wa2, b_a[l][None],
                     jnp.tile(q_norm[l], QK_W // DIFF_DK)[None], jnp.tile(k_norm[l], QK_W // DIFF_DK)[None])
        post_args = (jnp.tile(subln[l], DIFF_HEADS)[None], jnp.tile(gla_norm[l], GLA_HEADS)[None],
                     w_out_b, norm2[l][None], w_gu_b, w_down_b)

        q, kb, kt, vb, v4, gq, gk, gv, gl, gg = _proj_prompt(yp, *proj_args, cos_p, sin_p, layer=l, batch=batch,
                                                             tm=ROW_TILE, prev_kv=kv_prompt)
        kv_prompt = (kt, v4)
        q_s, k_s_l, v_s_l, gqt, gkt, gv_s, et, gg_s = _proj_sample(ys, *proj_args, cos_s, sin_s, layer=l,
                                                                   tb=GLA_SAMPLE_BLOCK)
        od = _attn_prompt(q, kb, vb, lambda_qk[l], batch=batch, seq=seq, tq=ATTN_Q_TILE, heads=ATTN_HEADS_PER_STEP,
                          lam_init=lam_init)
        og, s_fin = _gla_prompt(gq, gk, gv, gl, batch=batch, seq=seq, tg=GLA_STEP_TOKENS)
        yp, od_s = _post_attn(yp, od, og, gg, *post_args, page_table, q_s, k_s_l, v_s_l, lambda_qk[l], ck, cv,
                              tm=ROW_TILE, lam_init=lam_init, layer=l)
        sp.append(s_fin)

        og_s, states_s = _gla_sample(gqt, gkt, gv_s, et, state_gla, layer=l, prev_states=states_s)
        ys = _post(ys, od_s, og_s, gg_s, *post_args, layer=l, tm=nb, lam_init=lam_init)
        k_s.append(k_s_l.reshape(nb, dec_seq, DIFF_HEADS, 2, DIFF_DK))
        v_s.append(v_s_l.reshape(nb, dec_seq, DIFF_HEADS, DIFF_DV))

    k_prompt = kv_prompt[0].reshape(depth, batch, DIFF_HEADS, 2, DIFF_DK, seq).transpose(0, 1, 5, 2, 3, 4)
    v_prompt = kv_prompt[1].reshape(depth, batch, seq, DIFF_HEADS, DIFF_DV)
    return (yp.reshape(batch, seq, d), ys.reshape(nb, dec_seq, d), k_prompt, v_prompt, jnp.stack(sp),
            jnp.stack(k_s), jnp.stack(v_s), states_s)
```
